```python
import math
import jax, jax.numpy as jnp
from jax import lax
import numpy as np

D_MODEL = 1024
BATCH = 16
SEQ = 2048
DEPTH = 1
DEC_BATCH = 128
DEC_SEQ = 8
PAST_LEN = 8192
PAGE_SIZE = 128

ATT_WIDTH = D_MODEL // 2
SSM_WIDTH = D_MODEL - ATT_WIDTH
HEAD_DIM = 64
N_HEADS = ATT_WIDTH // HEAD_DIM
BRANCHES = ((128, 1), (512, 4), (2048, 16))
MAX_WINDOW = 2048
BLK = 128
SSM_GROUP = 16
N_SSM_GROUPS = SSM_WIDTH // SSM_GROUP
SSM_STATE = 64
DT_MIN = 0.001
DT_MAX = 0.1
PEER_HEADS = 8
N_KEYS = 128
N_EXPERTS = N_KEYS * N_KEYS
PEER_TOPK = 16
D_KEY = 256
PEER_BLOCK = 128
PLE_DIM = 256
ALPHA = (2 * DEPTH) ** 0.25
BETA = (8 * DEPTH) ** -0.25
LN_EPS = 1e-5

kernel_name = "hybrid_s5_dilated_peer_decoder_step"


def _alibi_slopes():
    return 2.0 ** (-8.0 * jnp.arange(1, N_HEADS + 1, dtype=jnp.float32) / N_HEADS)


def _layer_norm(x, g, b):
    xf = x.astype(jnp.float32)
    mu = jnp.mean(xf, -1, keepdims=True)
    var = jnp.mean(jnp.square(xf - mu), -1, keepdims=True)
    return ((xf - mu) * lax.rsqrt(var + LN_EPS)).astype(x.dtype) * g + b


def _rms_norm(x, g):
    xf = x.astype(jnp.float32)
    return (xf * lax.rsqrt(jnp.mean(xf * xf, -1, keepdims=True) + LN_EPS)).astype(x.dtype) * g


def _in_proj(x, w_in, b_in):
    B, T, _ = x.shape
    proj = x @ w_in + b_in
    q, k, v, u = jnp.split(proj, [ATT_WIDTH, 2 * ATT_WIDTH, 3 * ATT_WIDTH], axis=-1)
    heads = lambda t: t.reshape(B, T, N_HEADS, HEAD_DIM)
    return heads(q), heads(k), heads(v), u.reshape(B, T, N_SSM_GROUPS, SSM_GROUP)


def _dilated_branch_prompt(q, k, v, window, dilation, slopes):
    B, S, H, E = q.shape
    L = S // dilation
    steps = window // dilation
    nb = -(-L // BLK)
    Lp = nb * BLK

    def to_sub(t):
        t = t.reshape(B, L, dilation, H, E).transpose(0, 2, 1, 3, 4)
        t = jnp.pad(t, ((0, 0), (0, 0), (0, Lp - L), (0, 0), (0, 0)))
        return t.reshape(B, dilation, nb, BLK, H, E)

    def with_prev(t):
        prev = jnp.pad(t[:, :, :-1], ((0, 0), (0, 0), (1, 0), (0, 0), (0, 0), (0, 0)))
        return jnp.concatenate([prev, t], axis=3)

    qs = to_sub(q)
    kk = with_prev(to_sub(k))
    vv = with_prev(to_sub(v))
    s = jnp.einsum('bdnqhe,bdnkhe->bdnhqk', qs, kk).astype(jnp.float32) * (E ** -0.5)
    qi = jnp.arange(BLK)[:, None]
    kj = jnp.arange(2 * BLK)[None, :]
    delta = qi - kj + BLK
    jglob = jnp.arange(nb)[:, None, None] * BLK - BLK + kj[None]
    valid = (delta >= 0) & (delta <= steps) & (jglob >= 0)
    bias = -slopes[:, None, None] * (dilation * delta).astype(jnp.float32)
    s = jnp.where(valid[None, None, :, None], s + bias[None, None, None], -jnp.inf)
    lse = jax.nn.logsumexp(s, axis=-1)
    p = jnp.exp(s - lse[..., None])
    o = jnp.einsum('bdnhqk,bdnkhe->bdnqhe', p.astype(vv.dtype), vv)
    o = o.reshape(B, dilation, Lp, H, E)[:, :, :L].transpose(0, 2, 1, 3, 4).reshape(B, S, H, E)
    lse = lse.transpose(0, 1, 2, 4, 3).reshape(B, dilation, Lp, H)[:, :, :L]
    lse = lse.transpose(0, 2, 1, 3).reshape(B, S, H)
    return o, lse


def _dilated_branch_sample(q, kc, vc, window, dilation, slopes):
    B, T, H, E = q.shape
    W = kc.shape[1] - T
    steps = window // dilation
    m = jnp.arange(steps + 1)
    rows = W + jnp.arange(T)[:, None] - dilation * m[None, :]
    valid = rows >= 0
    rows = jnp.maximum(rows, 0)
    kg = jnp.take(kc, rows, axis=1)
    vg = jnp.take(vc, rows, axis=1)
    s = jnp.einsum('bthe,btmhe->bhtm', q, kg).astype(jnp.float32) * (E ** -0.5)
    bias = -slopes[:, None, None] * (dilation * m).astype(jnp.float32)[None, None, :]
    s = jnp.where(valid[None, None], s + bias, -jnp.inf)
    lse = jax.nn.logsumexp(s, axis=-1)
    p = jnp.exp(s - lse[..., None])
    o = jnp.einsum('bhtm,btmhe->bthe', p.astype(vg.dtype), vg)
    return o, lse.transpose(0, 2, 1)


def _merge_branches(outs, lses):
    w = jax.nn.softmax(jnp.stack(lses, 0), axis=0)
    return jnp.einsum('rbth,rbthe->bthe', w.astype(outs[0].dtype), jnp.stack(outs, 0))


def _attn_prompt(q, k, v):
    slopes = _alibi_slopes()
    outs, lses = [], []
    for window, dilation in BRANCHES:
        o, l = _dilated_branch_prompt(q, k, v, window, dilation, slopes)
        outs.append(o)
        lses.append(l)
    return _merge_branches(outs, lses)


def _attn_sample(q, kc, vc):
    slopes = _alibi_slopes()
    outs, lses = [], []
    for window, dilation in BRANCHES:
        o, l = _dilated_branch_sample(q, kc, vc, window, dilation, slopes)
        outs.append(o)
        lses.append(l)
    return _merge_branches(outs, lses)


def _ssm(u, h0_re, h0_im, lam_re, lam_im, log_dt, b_re, b_im, c_re, c_im, d_skip):
    f32 = jnp.float32
    dt = jnp.exp(log_dt.astype(f32))[:, None]
    lr, li = lam_re.astype(f32), lam_im.astype(f32)
    mag = jnp.exp(lr * dt)
    a_re = mag * jnp.cos(li * dt)
    a_im = mag * jnp.sin(li * dt)
    den = lr * lr + li * li
    nr = a_re - 1.0
    z_re = (nr * lr + a_im * li) / den
    z_im = (a_im * lr - nr * li) / den
    uf = u.astype(f32)
    bu_re = jnp.einsum('btgp,gnp->btgn', uf, b_re.astype(f32))
    bu_im = jnp.einsum('btgp,gnp->btgn', uf, b_im.astype(f32))
    x_re = z_re * bu_re - z_im * bu_im
    x_im = z_re * bu_im + z_im * bu_re
    h0r, h0i = h0_re.astype(f32), h0_im.astype(f32)
    x_re = x_re.at[:, 0].add(a_re * h0r - a_im * h0i)
    x_im = x_im.at[:, 0].add(a_re * h0i + a_im * h0r)

    def combine(e1, e2):
        a1r, a1i, b1r, b1i = e1
        a2r, a2i, b2r, b2i = e2
        return (a2r * a1r - a2i * a1i, a2r * a1i + a2i * a1r,
                a2r * b1r - a2i * b1i + b2r, a2r * b1i + a2i * b1r + b2i)

    elems = (jnp.broadcast_to(a_re, x_re.shape), jnp.broadcast_to(a_im, x_re.shape), x_re, x_im)
    _, _, h_re, h_im = lax.associative_scan(combine, elems, axis=1)
    y = (jnp.einsum('btgn,gpn->btgp', h_re, c_re.astype(f32))
         - jnp.einsum('btgn,gpn->btgp', h_im, c_im.astype(f32))
         + d_skip.astype(f32) * uf)
    return y, h_re[:, -1], h_im[:, -1]


def _peer(h, w_pq, sub_keys1, sub_keys2, peer_u, peer_v):
    B, T, D = h.shape
    x = h.reshape(B * T, D)
    n = x.shape[0]
    nblk = -(-n // PEER_BLOCK)
    x = jnp.pad(x, ((0, nblk * PEER_BLOCK - n), (0, 0))).reshape(nblk, PEER_BLOCK, D)

    def block(xb):
        q = (xb @ w_pq).reshape(PEER_BLOCK, PEER_HEADS, 2, D_KEY // 2)
        s1 = jnp.einsum('thk,nk->thn', q[:, :, 0], sub_keys1).astype(jnp.float32)
        s2 = jnp.einsum('thk,nk->thn', q[:, :, 1], sub_keys2).astype(jnp.float32)
        v1, i1 = lax.top_k(s1, PEER_TOPK)
        v2, i2 = lax.top_k(s2, PEER_TOPK)
        cand = (v1[..., :, None] + v2[..., None, :]).reshape(PEER_BLOCK, PEER_HEADS, PEER_TOPK * PEER_TOPK)
        sc, ci = lax.top_k(cand, PEER_TOPK)
        e = (jnp.take_along_axis(i1, ci // PEER_TOPK, -1) * N_KEYS
             + jnp.take_along_axis(i2, ci % PEER_TOPK, -1))
        g = jax.nn.softmax(sc, axis=-1)
        u = peer_u[e]
        v = peer_v[e]
        a = jax.nn.gelu(jnp.einsum('td,thkd->thk', xb, u).astype(jnp.float32))
        return jnp.einsum('thk,thkd->td', (g * a).astype(v.dtype), v)

    out = lax.map(block, x)
    return out.reshape(-1, D)[:n].reshape(B, T, D)


def _layer_tail(x, att, ssm_y, p, w_glu, b_glu, g_att_out, g_ssm_out, w_out, b_out,
                ln1_g, ln1_b, w_pq, sub_keys1, sub_keys2, peer_u, peer_v,
                w_ple, w_ple_gate, ln2_g, ln2_b):
    B, T, _ = x.shape
    z = jax.nn.gelu(ssm_y.reshape(B, T, SSM_WIDTH)).astype(x.dtype)
    ssm_out = z * jax.nn.sigmoid(z @ w_glu + b_glu)
    mixed = jnp.concatenate([_rms_norm(att.reshape(B, T, ATT_WIDTH), g_att_out),
                             _rms_norm(ssm_out, g_ssm_out)], axis=-1)
    h = _layer_norm(ALPHA * x + mixed @ w_out + b_out, ln1_g, ln1_b)
    ple = jax.nn.sigmoid(h @ w_ple_gate) * (p @ w_ple)
    ffn = _peer(h, w_pq, sub_keys1, sub_keys2, peer_u, peer_v)
    return _layer_norm(ALPHA * h + ffn + ple, ln2_g, ln2_b)


def setup_inputs(seed: int = 0) -> dict:
    key = jax.random.key(seed)
    ks = jax.random.split(key, 36)
    f32 = jnp.float32
    nrm = lambda i, shape, scale=1.0: scale * jax.random.normal(ks[i], shape, f32)
    w_buf = min(MAX_WINDOW, PAST_LEN)
    L = DEPTH
    col_scale = jnp.concatenate([jnp.ones((2 * ATT_WIDTH,), f32), jnp.full((ATT_WIDTH,), BETA, f32),
                                 jnp.ones((SSM_WIDTH,), f32)])
    return {
        "x_prompt": nrm(0, (BATCH, SEQ, D_MODEL)),
        "x_sample": nrm(1, (DEC_BATCH, DEC_SEQ, D_MODEL)),
        "cache_k": nrm(2, (L, DEC_BATCH, w_buf, N_HEADS, HEAD_DIM)),
        "cache_v": nrm(3, (L, DEC_BATCH, w_buf, N_HEADS, HEAD_DIM), BETA),
        "state_ssm_re": nrm(4, (L, DEC_BATCH, N_SSM_GROUPS, SSM_STATE), 0.5),
        "state_ssm_im": nrm(5, (L, DEC_BATCH, N_SSM_GROUPS, SSM_STATE), 0.5),
        "p_prompt": nrm(6, (L, BATCH, SEQ, PLE_DIM)),
        "p_sample": nrm(7, (L, DEC_BATCH, DEC_SEQ, PLE_DIM)),
        "w_in": nrm(8, (L, D_MODEL, 3 * ATT_WIDTH + SSM_WIDTH), D_MODEL ** -0.5) * col_scale,
        "b_in": nrm(9, (L, 3 * ATT_WIDTH + SSM_WIDTH), 0.02),
        "lam_re": -0.5 + nrm(10, (L, N_SSM_GROUPS, SSM_STATE), 0.01),
        "lam_im": jnp.pi * jnp.arange(SSM_STATE, dtype=f32) + nrm(11, (L, N_SSM_GROUPS, SSM_STATE), 0.01),
        "log_dt": jax.random.uniform(ks[12], (L, N_SSM_GROUPS), f32, math.log(DT_MIN), math.log(DT_MAX)),
        "ssm_b_re": nrm(13, (L, N_SSM_GROUPS, SSM_STATE, SSM_GROUP), (2 * SSM_GROUP) ** -0.5),
        "ssm_b_im": nrm(14, (L, N_SSM_GROUPS, SSM_STATE, SSM_GROUP), (2 * SSM_GROUP) ** -0.5),
        "ssm_c_re": nrm(15, (L, N_SSM_GROUPS, SSM_GROUP, SSM_STATE), (2 * SSM_STATE) ** -0.5),
        "ssm_c_im": nrm(16, (L, N_SSM_GROUPS, SSM_GROUP, SSM_STATE), (2 * SSM_STATE) ** -0.5),
        "ssm_d": nrm(17, (L, N_SSM_GROUPS, SSM_GROUP)),
        "w_glu": nrm(18, (L, SSM_WIDTH, SSM_WIDTH), SSM_WIDTH ** -0.5),
        "b_glu": nrm(19, (L, SSM_WIDTH), 0.02),
        "g_att_out": 1.0 + nrm(20, (L, ATT_WIDTH), 0.02),
        "g_ssm_out": 1.0 + nrm(21, (L, SSM_WIDTH), 0.02),
        "w_out": nrm(22, (L, D_MODEL, D_MODEL), BETA * D_MODEL ** -0.5),
        "b_out": nrm(23, (L, D_MODEL), 0.02),
        "ln1_g": 1.0 + nrm(24, (L, D_MODEL), 0.02),
        "ln1_b": nrm(25, (L, D_MODEL), 0.02),
        "w_pq": nrm(26, (L, D_MODEL, PEER_HEADS * D_KEY), D_MODEL ** -0.5),
        "sub_keys1": nrm(27, (L, N_KEYS, D_KEY // 2), (D_KEY // 2) ** -0.5),
        "sub_keys2": nrm(28, (L, N_KEYS, D_KEY // 2), (D_KEY // 2) ** -0.5),
        "peer_u": nrm(29, (L, N_EXPERTS, D_MODEL), D_MODEL ** -0.5),
        "peer_v": nrm(30, (L, N_EXPERTS, D_MODEL), BETA),
        "w_ple": nrm(31, (L, PLE_DIM, D_MODEL), PLE_DIM ** -0.5),
        "w_ple_gate": nrm(32, (L, D_MODEL, D_MODEL), D_MODEL ** -0.5),
        "ln2_g": 1.0 + nrm(33, (L, D_MODEL), 0.02),
        "ln2_b": nrm(34, (L, D_MODEL), 0.02),
    }


def reference(x_prompt, x_sample, cache_k, cache_v, state_ssm_re, state_ssm_im, p_prompt, p_sample,
              w_in, b_in, lam_re, lam_im, log_dt, ssm_b_re, ssm_b_im, ssm_c_re, ssm_c_im, ssm_d,
              w_glu, b_glu, g_att_out, g_ssm_out, w_out, b_out, ln1_g, ln1_b,
              w_pq, sub_keys1, sub_keys2, peer_u, peer_v, w_ple, w_ple_gate, ln2_g, ln2_b):
    yp, ys = x_prompt, x_sample
    w_keep = min(MAX_WINDOW, x_prompt.shape[1])
    kp_l, vp_l, srp_l, sip_l, ks_l, vs_l, srs_l, sis_l = [], [], [], [], [], [], [], []
    for i in range(DEPTH):
        ssm_params = (lam_re[i], lam_im[i], log_dt[i], ssm_b_re[i], ssm_b_im[i],
                      ssm_c_re[i], ssm_c_im[i], ssm_d[i])
        tail_params = (w_glu[i], b_glu[i], g_att_out[i], g_ssm_out[i], w_out[i], b_out[i],
                       ln1_g[i], ln1_b[i], w_pq[i], sub_keys1[i], sub_keys2[i], peer_u[i], peer_v[i],
                       w_ple[i], w_ple_gate[i], ln2_g[i], ln2_b[i])
        q, k, v, u = _in_proj(yp, w_in[i], b_in[i])
        att = _attn_prompt(q, k, v)
        h0 = jnp.zeros((yp.shape[0], N_SSM_GROUPS, SSM_STATE), jnp.float32)
        sy, hr, hi = _ssm(u, h0, h0, *ssm_params)
        kp_l.append(k[:, -w_keep:])
        vp_l.append(v[:, -w_keep:])
        srp_l.append(hr)
        sip_l.append(hi)
        yp = _layer_tail(yp, att, sy, p_prompt[i], *tail_params)
        q, k, v, u = _in_proj(ys, w_in[i], b_in[i])
        kc = jnp.concatenate([cache_k[i].astype(k.dtype), k], axis=1)
        vc = jnp.concatenate([cache_v[i].astype(v.dtype), v], axis=1)
        att = _attn_sample(q, kc, vc)
        sy, hr, hi = _ssm(u, state_ssm_re[i], state_ssm_im[i], *ssm_params)
        ks_l.append(k)
        vs_l.append(v)
        srs_l.append(hr)
        sis_l.append(hi)
        ys = _layer_tail(ys, att, sy, p_sample[i], *tail_params)
    return (yp, ys, jnp.stack(kp_l), jnp.stack(vp_l), jnp.stack(srp_l), jnp.stack(sip_l),
            jnp.stack(ks_l), jnp.stack(vs_l), jnp.stack(srs_l), jnp.stack(sis_l))
```

```python
import functools
import math

import jax
import jax.numpy as jnp
from jax import lax
from jax.experimental import pallas as pl
from jax.experimental.pallas import tpu as pltpu

D_MODEL = 1024
ATT_WIDTH = 512
SSM_WIDTH = 512
HEAD_DIM = 64
N_HEADS = 8
BRANCHES = ((128, 1), (512, 4), (2048, 16))
BLK = 128
SSM_GROUP = 16
N_SSM_GROUPS = 32
SSM_STATE = 64
SSM_LANES = N_SSM_GROUPS * SSM_STATE
PEER_HEADS = 8
N_KEYS = 128
N_EXPERTS = N_KEYS * N_KEYS
PEER_TOPK = 16
D_KEY = 256
PLE_DIM = 256
DEPTH = 1
ALPHA = (2 * DEPTH) ** 0.25
LN_EPS = 1e-5

LANES = 128
SUBLANES = 8
NEG_BIG = -1e30
VMEM_LIMIT = 56 * 1024 * 1024

F32 = jnp.float32
BF16 = jnp.bfloat16

_NT = (((1,), (1,)), ((), ()))


def _cparams(*sem):
    return pltpu.CompilerParams(dimension_semantics=sem, vmem_limit_bytes=VMEM_LIMIT)


def _gelu(x):
    c = math.sqrt(2.0 / math.pi)
    return 0.5 * x * (1.0 + jnp.tanh(c * (x + 0.044715 * (x * x * x))))


def _sigmoid(x):
    return 1.0 / (1.0 + jnp.exp(-x))


def _in_proj_kernel(x_ref, w_ref, b_ref, q_ref, k_ref, v_ref, u_ref):
    x = x_ref[...].astype(BF16)
    proj = jnp.dot(x, w_ref[...], preferred_element_type=F32) + b_ref[...]
    q_ref[...] = proj[:, :ATT_WIDTH] * (HEAD_DIM ** -0.5)
    k_ref[...] = proj[:, ATT_WIDTH:2 * ATT_WIDTH]
    v_ref[...] = proj[:, 2 * ATT_WIDTH:3 * ATT_WIDTH]
    u_ref[...] = proj[:, 3 * ATT_WIDTH:]


def _in_proj(x3, w_bf, b2, time_major, tm=512):
    B, S, D = x3.shape
    n_out = w_bf.shape[1]
    tm = min(tm, S)
    spec = pl.BlockSpec((None, tm, ATT_WIDTH), lambda b, t: (b, t, 0))
    if time_major:
        u_shape = (B // SUBLANES, S, SUBLANES * SSM_WIDTH)
        u_spec = pl.BlockSpec((None, tm, SSM_WIDTH), lambda b, t: (b // SUBLANES, t, b % SUBLANES))
    else:
        u_shape = (B, S, SSM_WIDTH)
        u_spec = spec
    sds = jax.ShapeDtypeStruct((B, S, ATT_WIDTH), F32)
    return pl.pallas_call(
        _in_proj_kernel,
        grid=(B, S // tm),
        in_specs=[pl.BlockSpec((None, tm, D), lambda b, t: (b, t, 0)),
                  pl.BlockSpec((D, n_out), lambda b, t: (0, 0)),
                  pl.BlockSpec((1, n_out), lambda b, t: (0, 0))],
        out_specs=[spec, spec, spec, u_spec],
        out_shape=[sds, sds, sds, jax.ShapeDtypeStruct(u_shape, F32)],
        compiler_params=_cparams("parallel", "parallel"),
        name="in_proj",
    )(x3, w_bf, b2)


def _alibi_slopes():
    return [2.0 ** (-8.0 * (h + 1) / N_HEADS) for h in range(N_HEADS)]


def _prompt_bias_table():
    slopes = jnp.asarray(_alibi_slopes(), F32).reshape(N_HEADS // 2, 1, 2, 1, 1)
    qi = jnp.arange(BLK)[:, None]
    kj = jnp.arange(2 * BLK)[None, :]
    delta = qi - kj + BLK
    tabs = []
    for window, dil in BRANCHES:
        steps = window // dil
        valid = (delta >= 0) & (delta <= steps)
        dist = (dil * delta).astype(F32)
        tabs.append(jnp.where(valid, -slopes * dist, NEG_BIG))
    return jnp.concatenate(tabs, axis=1)


def _attn_prompt_kernel(q_ref, k_ref, v_ref, bias_ref, o_ref, acc_scr, m_scr, l_scr, *, seq):
    lane = lax.broadcasted_iota(jnp.int32, (BLK, LANES), 1)
    is_a = lane < HEAD_DIM

    for br, (_, dil) in enumerate(BRANCHES):
        nb = (seq // dil) // BLK

        def block_step(idx, carry, br=br, dil=dil, nb=nb):
            r = idx // nb
            n = idx % nb
            start = n * (BLK * dil) + r
            pstart = jnp.maximum(n - 1, 0) * (BLK * dil) + r
            if dil == 1:
                rows = pl.ds(pl.multiple_of(start, BLK), BLK)
                prows = pl.ds(pl.multiple_of(pstart, BLK), BLK)
            else:
                rows = pl.ds(start, BLK, stride=dil)
                prows = pl.ds(pstart, BLK, stride=dil)
            qb = q_ref[rows, :]
            kk = jnp.concatenate([k_ref[prows, :], k_ref[rows, :]], axis=0).astype(BF16)
            vv = jnp.concatenate([v_ref[prows, :], v_ref[rows, :]], axis=0).astype(BF16)
            pen = jnp.where(n == 0, NEG_BIG, 0.0).astype(F32)
            pen_tile = jnp.concatenate([jnp.full((BLK, BLK), pen, F32), jnp.zeros((BLK, BLK), F32)], axis=1)
            outs = []
            for hh in range(2):
                qh = jnp.where(is_a if hh == 0 else jnp.logical_not(is_a), qb, 0.0).astype(BF16)
                s = lax.dot_general(qh, kk, _NT, preferred_element_type=F32)
                s = s + bias_ref[br, hh] + pen_tile
                m = jnp.max(s, axis=-1, keepdims=True)
                p = jnp.exp(s - m)
                l = jnp.sum(p, axis=-1, keepdims=True)
                acc = jnp.dot(p.astype(BF16), vv, preferred_element_type=F32)
                outs.append((acc, m, l))
            (acc_a, m_a, l_a), (acc_b, m_b, l_b) = outs
            acc_scr[br, rows, :] = jnp.where(is_a, acc_a, acc_b)
            m_scr[br, rows, :] = jnp.where(is_a, m_a, m_b)
            l_scr[br, rows, :] = jnp.where(is_a, l_a, l_b)
            return carry

        lax.fori_loop(0, dil * nb, block_step, 0)

    chunk = 256
    for c0 in range(0, seq, chunk):
        sl = pl.ds(c0, chunk)
        m1, m2, m3 = m_scr[0, sl, :], m_scr[1, sl, :], m_scr[2, sl, :]
        mm = jnp.maximum(jnp.maximum(m1, m2), m3)
        w1, w2, w3 = jnp.exp(m1 - mm), jnp.exp(m2 - mm), jnp.exp(m3 - mm)
        num = w1 * acc_scr[0, sl, :] + w2 * acc_scr[1, sl, :] + w3 * acc_scr[2, sl, :]
        den = w1 * l_scr[0, sl, :] + w2 * l_scr[1, sl, :] + w3 * l_scr[2, sl, :]
        o_ref[sl, :] = num / den


def _attn_prompt(q, k, v):
    B, S, _ = q.shape
    assert S == BRANCHES[-1][1] * BLK, "dilated blocks assume SEQ == 16 * 128"
    bias = _prompt_bias_table()
    spec = pl.BlockSpec((None, S, LANES), lambda b, hp: (b, 0, hp))
    scr = pltpu.VMEM((len(BRANCHES), S, LANES), F32)
    return pl.pallas_call(
        functools.partial(_attn_prompt_kernel, seq=S),
        grid=(B, ATT_WIDTH // LANES),
        in_specs=[spec, spec, spec,
                  pl.BlockSpec((None, len(BRANCHES), 2, BLK, 2 * BLK), lambda b, hp: (hp, 0, 0, 0, 0))],
        out_specs=spec,
        out_shape=jax.ShapeDtypeStruct((B, S, ATT_WIDTH), F32),
        scratch_shapes=[scr, scr, scr],
        compiler_params=_cparams("parallel", "parallel"),
        name="attn_prompt",
    )(q, k, v, bias)


def _attn_sample_kernel(q_ref, kn_ref, vn_ref, kc_ref, vc_ref, o_ref, *, w_buf, t_new):
    slopes = _alibi_slopes()
    lane = lax.broadcasted_iota(jnp.int32, (t_new, LANES), 1)
    is_a = lane < HEAD_DIM
    rows2 = 2 * t_new
    t_c = lax.broadcasted_iota(jnp.int32, (rows2, w_buf), 0) % t_new
    col_c = lax.broadcasted_iota(jnp.int32, (rows2, w_buf), 1)
    delta_c = w_buf + t_c - col_c
    t_n = lax.broadcasted_iota(jnp.int32, (rows2, rows2), 0) % t_new
    col_n = lax.broadcasted_iota(jnp.int32, (rows2, rows2), 1)
    delta_n = t_n - col_n
    head_b_c = lax.broadcasted_iota(jnp.int32, (rows2, w_buf), 0) >= t_new
    head_b_n = lax.broadcasted_iota(jnp.int32, (rows2, rows2), 0) >= t_new

    for hp in range(ATT_WIDTH // LANES):
        cs = slice(hp * LANES, (hp + 1) * LANES)
        q = q_ref[:, cs]
        q2 = jnp.concatenate([jnp.where(is_a, q, 0.0), jnp.where(is_a, 0.0, q)], axis=0).astype(BF16)
        kc = kc_ref[:, cs].astype(BF16)
        vc = vc_ref[:, cs].astype(BF16)
        zpad = jnp.zeros((t_new, LANES), F32)
        kn = jnp.concatenate([kn_ref[:, cs], zpad], axis=0).astype(BF16)
        vn = jnp.concatenate([vn_ref[:, cs], zpad], axis=0).astype(BF16)
        s_c = lax.dot_general(q2, kc, _NT, preferred_element_type=F32)
        s_n = lax.dot_general(q2, kn, _NT, preferred_element_type=F32)
        slope_c = jnp.where(head_b_c, slopes[2 * hp + 1], slopes[2 * hp]).astype(F32)
        slope_n = jnp.where(head_b_n, slopes[2 * hp + 1], slopes[2 * hp]).astype(F32)
        sb_c, sb_n = [], []
        for window, dil in BRANCHES:
            ok_c = (delta_c <= window) & ((delta_c & (dil - 1)) == 0)
            ok_n = (delta_n >= 0) & (col_n < t_new) & (delta_n <= window) & ((delta_n & (dil - 1)) == 0)
            sb_c.append(jnp.where(ok_c, s_c - slope_c * delta_c.astype(F32), NEG_BIG))
            sb_n.append(jnp.where(ok_n, s_n - slope_n * delta_n.astype(F32), NEG_BIG))
        mm = None
        for a in sb_c + sb_n:
            am = jnp.max(a, axis=-1, keepdims=True)
            mm = am if mm is None else jnp.maximum(mm, am)
        p_c = sum(jnp.exp(a - mm) for a in sb_c)
        p_n = sum(jnp.exp(a - mm) for a in sb_n)
        den = jnp.sum(p_c, axis=-1, keepdims=True) + jnp.sum(p_n, axis=-1, keepdims=True)
        acc = (jnp.dot(p_c.astype(BF16), vc, preferred_element_type=F32)
               + jnp.dot(p_n.astype(BF16), vn, preferred_element_type=F32))
        out2 = acc / den
        o_ref[:, cs] = jnp.where(is_a, out2[:t_new], out2[t_new:])


def _attn_sample(q, kn, vn, kc, vc):
    B, T, _ = q.shape
    W = kc.shape[1]
    assert W >= BRANCHES[-1][0], "every dilated key of a sample query must lie inside the window buffer"
    new = pl.BlockSpec((None, T, ATT_WIDTH), lambda b: (b, 0, 0))
    cache = pl.BlockSpec((None, W, ATT_WIDTH), lambda b: (b, 0, 0))
    return pl.pallas_call(
        functools.partial(_attn_sample_kernel, w_buf=W, t_new=T),
        grid=(B,),
        in_specs=[new, new, new, cache, cache],
        out_specs=new,
        out_shape=jax.ShapeDtypeStruct((B, T, ATT_WIDTH), F32),
        compiler_params=_cparams("parallel"),
        name="attn_sample",
    )(q, kn, vn, kc, vc)


def _ssm_prep_kernel(lr_ref, li_ref, ldt_ref, btr_ref, bti_ref, ctr_ref, cti_ref, a_ref, bz_ref, cb_ref):
    lr, li = lr_ref[...], li_ref[...]
    dt = jnp.exp(ldt_ref[...])
    mag = jnp.exp(lr * dt)
    a_re = mag * jnp.cos(li * dt)
    a_im = mag * jnp.sin(li * dt)
    den = lr * lr + li * li
    nr = a_re - 1.0
    z_re = (nr * lr + a_im * li) / den
    z_im = (a_im * lr - nr * li) / den
    a_ref[0] = jnp.broadcast_to(a_re, (SUBLANES, SSM_LANES))
    a_ref[1] = jnp.broadcast_to(a_im, (SUBLANES, SSM_LANES))
    btr, bti = btr_ref[...], bti_ref[...]
    bz_re = z_re * btr - z_im * bti
    bz_im = z_re * bti + z_im * btr
    row_g = lax.broadcasted_iota(jnp.int32, (SSM_WIDTH, SSM_LANES), 0) // SSM_GROUP
    col_g = lax.broadcasted_iota(jnp.int32, (SSM_WIDTH, SSM_LANES), 1) // SSM_STATE
    diag = row_g == col_g

    def block_diag(small):
        tiled = jnp.concatenate([small] * N_SSM_GROUPS, axis=0)
        return jnp.where(diag, tiled, 0.0).astype(BF16)

    bz_ref[0] = block_diag(bz_re)
    bz_ref[1] = block_diag(bz_im)
    cb_ref[0] = block_diag(ctr_ref[...])
    cb_ref[1] = block_diag(cti_ref[...])


def _ssm_prep(lam_re, lam_im, log_dt, b_re, b_im, c_re, c_im):
    row = lambda t: t.reshape(1, SSM_LANES)
    chan = lambda t: t.reshape(SSM_GROUP, SSM_LANES)
    args = (row(lam_re), row(lam_im), row(jnp.repeat(log_dt, SSM_STATE)),
            chan(b_re.transpose(2, 0, 1)), chan(b_im.transpose(2, 0, 1)),
            chan(c_re.transpose(1, 0, 2)), chan(c_im.transpose(1, 0, 2)))
    return pl.pallas_call(
        _ssm_prep_kernel,
        out_shape=[jax.ShapeDtypeStruct((2, SUBLANES, SSM_LANES), F32),
                   jax.ShapeDtypeStruct((2, SSM_WIDTH, SSM_LANES), BF16),
                   jax.ShapeDtypeStruct((2, SSM_WIDTH, SSM_LANES), BF16)],
        compiler_params=pltpu.CompilerParams(vmem_limit_bytes=VMEM_LIMIT),
        name="ssm_prep",
    )(*args)


_SSM_LANE_CHUNK = 512


def _ssm_prompt_kernel(u_ref, a_ref, bz_ref, cb_ref, d_ref, z_ref, hre_ref, him_ref, xr, xi, hst, *, tc):
    @pl.when(pl.program_id(1) == 0)
    def _():
        hst[...] = jnp.zeros_like(hst)

    u = u_ref[...]
    ub = u.astype(BF16)
    xr[...] = jnp.dot(ub, bz_ref[0], preferred_element_type=F32)
    xi[...] = jnp.dot(ub, bz_ref[1], preferred_element_type=F32)
    for lc in range(SSM_LANES // _SSM_LANE_CHUNK):
        sl = slice(lc * _SSM_LANE_CHUNK, (lc + 1) * _SSM_LANE_CHUNK)
        ar, ai = a_ref[0, :, sl], a_ref[1, :, sl]

        def step(t, carry, sl=sl, ar=ar, ai=ai):
            hr, hi = carry
            rows = pl.ds(pl.multiple_of(t * SUBLANES, SUBLANES), SUBLANES)
            nhr = ar * hr - ai * hi + xr[rows, sl]
            nhi = ar * hi + ai * hr + xi[rows, sl]
            xr[rows, sl] = nhr
            xi[rows, sl] = nhi
            return nhr, nhi

        hr, hi = lax.fori_loop(0, tc, step, (hst[0, :, sl], hst[1, :, sl]))
        hst[0, :, sl] = hr
        hst[1, :, sl] = hi
    y = (lax.dot_general(xr[...].astype(BF16), cb_ref[0], _NT, preferred_element_type=F32)
         - lax.dot_general(xi[...].astype(BF16), cb_ref[1], _NT, preferred_element_type=F32)
         + d_ref[...] * u)
    z_ref[...] = _gelu(y)
    hre_ref[...] = hst[0]
    him_ref[...] = hst[1]


def _ssm_prompt(u_tm, a, bz, cb, d_row, tc=64):
    G, rows_total, _ = u_tm.shape
    S = rows_total // SUBLANES
    rows = tc * SUBLANES
    full3 = lambda shp: pl.BlockSpec(shp, lambda g, t: (0, 0, 0))
    st_spec = pl.BlockSpec((SUBLANES, SSM_LANES), lambda g, t: (g, 0))
    st_sds = jax.ShapeDtypeStruct((G * SUBLANES, SSM_LANES), F32)
    return pl.pallas_call(
        functools.partial(_ssm_prompt_kernel, tc=tc),
        grid=(G, S // tc),
        in_specs=[pl.BlockSpec((None, rows, SSM_WIDTH), lambda g, t: (g, t, 0)),
                  full3((2, SUBLANES, SSM_LANES)), full3((2, SSM_WIDTH, SSM_LANES)),
                  full3((2, SSM_WIDTH, SSM_LANES)),
                  pl.BlockSpec((1, SSM_WIDTH), lambda g, t: (0, 0))],
        out_specs=[pl.BlockSpec((None, rows, SSM_WIDTH), lambda g, t: (g, t, 0)), st_spec, st_spec],
        out_shape=[jax.ShapeDtypeStruct(u_tm.shape, F32), st_sds, st_sds],
        scratch_shapes=[pltpu.VMEM((rows, SSM_LANES), F32), pltpu.VMEM((rows, SSM_LANES), F32),
                        pltpu.VMEM((2, SUBLANES, SSM_LANES), F32)],
        compiler_params=_cparams("parallel", "arbitrary"),
        name="ssm_prompt",
    )(u_tm, a, bz, cb, d_row)


def _ssm_sample_kernel(u_ref, h0r_ref, h0i_ref, a_ref, bz_ref, cb_ref, d_ref, z_ref, hre_ref, him_ref,
                       xr, xi, *, t_new):
    u = u_ref[...]
    ub = u.astype(BF16)
    nb = h0r_ref.shape[0]
    n_tiles = SSM_LANES // LANES
    x_re = jnp.dot(ub, bz_ref[0], preferred_element_type=F32)
    x_im = jnp.dot(ub, bz_ref[1], preferred_element_type=F32)
    for j in range(n_tiles):
        xr[j] = x_re[:, j * LANES:(j + 1) * LANES]
        xi[j] = x_im[:, j * LANES:(j + 1) * LANES]
    for j in range(n_tiles):
        cs = slice(j * LANES, (j + 1) * LANES)
        ar = jnp.broadcast_to(a_ref[0, 0:1, cs], (nb, LANES))
        ai = jnp.broadcast_to(a_ref[1, 0:1, cs], (nb, LANES))
        hr, hi = h0r_ref[:, cs], h0i_ref[:, cs]
        for t in range(t_new):
            rows = pl.ds(t, nb, stride=t_new)
            nhr = ar * hr - ai * hi + xr[j, rows, :]
            nhi = ar * hi + ai * hr + xi[j, rows, :]
            xr[j, rows, :] = nhr
            xi[j, rows, :] = nhi
            hr, hi = nhr, nhi
        hre_ref[:, cs] = hr
        him_ref[:, cs] = hi
    h_re = jnp.concatenate([xr[j] for j in range(n_tiles)], axis=1).astype(BF16)
    h_im = jnp.concatenate([xi[j] for j in range(n_tiles)], axis=1).astype(BF16)
    y = (lax.dot_general(h_re, cb_ref[0], _NT, preferred_element_type=F32)
         - lax.dot_general(h_im, cb_ref[1], _NT, preferred_element_type=F32)
         + d_ref[...] * u)
    z_ref[...] = _gelu(y)


def _ssm_sample(u, h0r, h0i, a, bz, cb, d_row, t_new):
    n_tok = u.shape[0]
    nb = h0r.shape[0]
    slab = pltpu.VMEM((SSM_LANES // LANES, n_tok, LANES), F32)
    st = jax.ShapeDtypeStruct((nb, SSM_LANES), F32)
    return pl.pallas_call(
        functools.partial(_ssm_sample_kernel, t_new=t_new),
        out_shape=[jax.ShapeDtypeStruct((n_tok, SSM_WIDTH), F32), st, st],
        scratch_shapes=[slab, slab],
        compiler_params=pltpu.CompilerParams(vmem_limit_bytes=VMEM_LIMIT),
        name="ssm_sample",
    )(u, h0r, h0i, a, bz, cb, d_row)


def _tail_kernel(x_ref, att_ref, z_ref, p_ref, wglu_ref, bglu_ref, gatt_ref, gssm_ref, wout_ref, bout_ref,
                 g1_ref, b1_ref, wple_ref, wgate_ref, h_ref, hb_ref, ple_ref):
    z = z_ref[...]
    gate = jnp.dot(z.astype(BF16), wglu_ref[...], preferred_element_type=F32) + bglu_ref[...]
    ssm_out = z * _sigmoid(gate)
    att = att_ref[...]
    rms_a = att * lax.rsqrt(jnp.mean(att * att, axis=-1, keepdims=True) + LN_EPS) * gatt_ref[...]
    rms_s = ssm_out * lax.rsqrt(jnp.mean(ssm_out * ssm_out, axis=-1, keepdims=True) + LN_EPS) * gssm_ref[...]
    mix = (jnp.dot(rms_a.astype(BF16), wout_ref[:ATT_WIDTH, :], preferred_element_type=F32)
           + jnp.dot(rms_s.astype(BF16), wout_ref[ATT_WIDTH:, :], preferred_element_type=F32)
           + bout_ref[...])
    pre = ALPHA * x_ref[...] + mix
    mu = jnp.mean(pre, axis=-1, keepdims=True)
    cen = pre - mu
    var = jnp.mean(cen * cen, axis=-1, keepdims=True)
    h = cen * lax.rsqrt(var + LN_EPS) * g1_ref[...] + b1_ref[...]
    hb = h.astype(BF16)
    h_ref[...] = h
    hb_ref[...] = hb
    gate2 = _sigmoid(jnp.dot(hb, wgate_ref[...], preferred_element_type=F32))
    ple_ref[...] = gate2 * jnp.dot(p_ref[...].astype(BF16), wple_ref[...], preferred_element_type=F32)


def _tail(x3, att3, z_arr, p3, weights, z_time_major, tm=256):
    B, S, D = x3.shape
    tm = min(tm, S)
    tok = lambda w: pl.BlockSpec((None, tm, w), lambda b, t: (b, t, 0))
    if z_time_major:
        z_arr = z_arr.reshape(B // SUBLANES, S, SUBLANES * SSM_WIDTH)
        z_spec = pl.BlockSpec((None, tm, SSM_WIDTH), lambda b, t: (b // SUBLANES, t, b % SUBLANES))
    else:
        z_spec = tok(SSM_WIDTH)
    full = lambda a: pl.BlockSpec(a.shape, lambda b, t: (0, 0))
    return pl.pallas_call(
        _tail_kernel,
        grid=(B, S // tm),
        in_specs=[tok(D), tok(ATT_WIDTH), z_spec, tok(PLE_DIM)] + [full(w) for w in weights],
        out_specs=[tok(D), tok(D), tok(D)],
        out_shape=[jax.ShapeDtypeStruct((B, S, D), F32), jax.ShapeDtypeStruct((B, S, D), BF16),
                   jax.ShapeDtypeStruct((B, S, D), F32)],
        compiler_params=_cparams("parallel", "parallel"),
        name="tail",
    )(x3, att3, z_arr, p3, *weights)


def _split_bf16(x):
    hi = x.astype(BF16)
    lo = (x - hi.astype(F32)).astype(BF16)
    return hi, lo


def _peer_prep_kernel(wpq_ref, sk1_ref, sk2_ref, m1_ref, m2_ref):
    def combined(sk, w):
        sk_hi, sk_lo = _split_bf16(sk)
        w_hi, w_lo = _split_bf16(w)
        dot = lambda a, b: lax.dot_general(a, b, _NT, preferred_element_type=F32)
        return dot(sk_hi, w_hi) + (dot(sk_hi, w_lo) + dot(sk_lo, w_hi))

    half = D_KEY // 2
    m1_ref[...] = combined(sk1_ref[...], wpq_ref[:, :half]).astype(BF16)
    m2_ref[...] = combined(sk2_ref[...], wpq_ref[:, half:]).astype(BF16)


def _peer_prep(w_pq, sk1, sk2):
    half = D_KEY // 2
    m1, m2 = pl.pallas_call(
        _peer_prep_kernel,
        grid=(PEER_HEADS,),
        in_specs=[pl.BlockSpec((D_MODEL, D_KEY), lambda h: (0, h)),
                  pl.BlockSpec((N_KEYS, half), lambda h: (0, 0)),
                  pl.BlockSpec((N_KEYS, half), lambda h: (0, 0))],
        out_specs=[pl.BlockSpec((N_KEYS, D_MODEL), lambda h: (0, h)),
                   pl.BlockSpec((None, N_KEYS, D_MODEL), lambda h: (h, 0, 0))],
        out_shape=[jax.ShapeDtypeStruct((N_KEYS, PEER_HEADS * D_MODEL), BF16),
                   jax.ShapeDtypeStruct((PEER_HEADS, N_KEYS, D_MODEL), BF16)],
        compiler_params=_cparams("parallel"),
        name="peer_prep",
    )(w_pq, sk1, sk2)
    return jnp.concatenate([m1.reshape(N_KEYS * PEER_HEADS, D_MODEL),
                            m2.reshape(PEER_HEADS * N_KEYS, D_MODEL)], axis=0)


def _desc_distinct(load, n_items, rounds):
    outs, prev = [], None
    for _ in range(rounds):
        cur = None
        for i in range(n_items):
            x = load(i)
            if prev is not None:
                x = jnp.where(x < prev, x, -1.0)
            cur = x if cur is None else jnp.maximum(cur, x)
        outs.append(cur)
        prev = cur
    return [jnp.maximum(o, 0.0) for o in outs]


_CAND_PAIRS = [(a, b) for a in range(PEER_TOPK) for b in range(PEER_TOPK) if (a + 1) * (b + 1) <= PEER_TOPK]


def _kth_largest(vals, k):
    prev = None
    for _ in range(k):
        cur = None
        for x in vals:
            if prev is not None:
                x = jnp.where(x < prev, x, -1.0)
            cur = x if cur is None else jnp.maximum(cur, x)
        prev = cur
    return prev


def _peer_gate_kernel(hb_ref, mt_ref, e1_ref, e2_ref, tau_ref, *, tn):
    st = lax.dot_general(mt_ref[...], hb_ref[...], _NT, preferred_element_type=F32)
    n1 = N_KEYS * PEER_HEADS
    rowid = lax.broadcasted_iota(jnp.int32, (SUBLANES, LANES), 0)
    for jt in range(tn // LANES):
        cs = slice(jt * LANES, (jt + 1) * LANES)
        s1 = [st[i * PEER_HEADS:(i + 1) * PEER_HEADS, cs] for i in range(N_KEYS)]
        max1 = functools.reduce(jnp.maximum, s1)
        for i in range(N_KEYS):
            e1_ref[jt, i * PEER_HEADS:(i + 1) * PEER_HEADS, :] = jnp.exp(s1[i] - max1)
        v1 = _desc_distinct(lambda i: e1_ref[jt, i * PEER_HEADS:(i + 1) * PEER_HEADS, :], N_KEYS, PEER_TOPK)
        e2_heads, v2_heads = [], []
        for h in range(PEER_HEADS):
            s2 = st[n1 + h * N_KEYS:n1 + (h + 1) * N_KEYS, cs]
            e2 = jnp.exp(s2 - jnp.max(s2, axis=0, keepdims=True))
            e2_heads.append(e2)
            vh, prev = [], None
            for _ in range(PEER_TOPK):
                xm = e2 if prev is None else jnp.where(e2 < prev, e2, -1.0)
                prev = jnp.max(xm, axis=0, keepdims=True)
                vh.append(jnp.maximum(prev, 0.0))
            v2_heads.append(vh)
        v2 = []
        for b in range(PEER_TOPK):
            acc = jnp.zeros((SUBLANES, LANES), F32)
            for h in range(PEER_HEADS):
                acc = jnp.where(rowid == h, v2_heads[h][b], acc)
            v2.append(acc)
        cand0 = [v1[a] * v2[b] for a, b in _CAND_PAIRS]
        tau0 = _kth_largest(cand0, PEER_TOPK)
        zsum = functools.reduce(lambda x, y: x + y, [jnp.where(c >= tau0, c, 0.0) for c in cand0])
        rz = 1.0 / zsum
        v2n = [v * rz for v in v2]
        cand = [v1[a] * v2n[b] for a, b in _CAND_PAIRS]
        tau_ref[jt] = _kth_largest(cand, PEER_TOPK)
        for h in range(PEER_HEADS):
            e2_ref[jt, h * N_KEYS:(h + 1) * N_KEYS, :] = e2_heads[h] * rz[h:h + 1, :]


def _peer_gate(hb2, mt, tn=256):
    n_tok = hb2.shape[0]
    tn = min(tn, n_tok)
    nt = tn // LANES
    slab = lambda rows: pl.BlockSpec((nt, rows, LANES), lambda t: (t, 0, 0))
    return pl.pallas_call(
        functools.partial(_peer_gate_kernel, tn=tn),
        grid=(n_tok // tn,),
        in_specs=[pl.BlockSpec((tn, D_MODEL), lambda t: (t, 0)),
                  pl.BlockSpec(mt.shape, lambda t: (0, 0))],
        out_specs=[slab(N_KEYS * PEER_HEADS), slab(N_KEYS * PEER_HEADS), slab(SUBLANES)],
        out_shape=[jax.ShapeDtypeStruct((n_tok // LANES, N_KEYS * PEER_HEADS, LANES), F32),
                   jax.ShapeDtypeStruct((n_tok // LANES, N_KEYS * PEER_HEADS, LANES), F32),
                   jax.ShapeDtypeStruct((n_tok // LANES, SUBLANES, LANES), F32)],
        compiler_params=_cparams("parallel"),
        name="peer_gate",
    )(hb2, mt)


def _peer_dense_kernel(hb_ref, u_ref, vt_ref, e1_ref, e2_ref, tau_ref, o_ref, acc_t, a_scr, w_scr, *, tb, ec):
    c = pl.program_id(1)
    n_lane_tiles = tb // LANES
    keys_per_chunk = ec // N_KEYS

    @pl.when(c == 0)
    def _():
        acc_t[...] = jnp.zeros_like(acc_t)

    a_t = lax.dot_general(u_ref[...], hb_ref[...], _NT, preferred_element_type=F32)
    for j in range(n_lane_tiles):
        a_scr[j] = a_t[:, j * LANES:(j + 1) * LANES]

    def tile_step(it, carry):
        j = it // keys_per_chunk
        k1_local = it % keys_per_chunk
        k1 = c * keys_per_chunk + k1_local
        e1b = [jnp.broadcast_to(e1_ref[j, pl.ds(k1 * PEER_HEADS + h, 1), :], (SUBLANES, LANES))
               for h in range(PEER_HEADS)]
        taub = [jnp.broadcast_to(tau_ref[j, pl.ds(h, 1), :], (SUBLANES, LANES)) for h in range(PEER_HEADS)]
        base = pl.multiple_of(k1_local * N_KEYS, N_KEYS)
        for kb in range(N_KEYS // SUBLANES):
            gate = jnp.zeros((SUBLANES, LANES), F32)
            for h in range(PEER_HEADS):
                prod = e1b[h] * e2_ref[j, pl.ds(h * N_KEYS + kb * SUBLANES, SUBLANES), :]
                gate = gate + jnp.where(prod >= taub[h], prod, 0.0)
            rows = pl.ds(base + kb * SUBLANES, SUBLANES)
            w_scr[j, rows, :] = gate * _gelu(a_scr[j, rows, :])
        return carry

    lax.fori_loop(0, n_lane_tiles * keys_per_chunk, tile_step, 0)
    w_t = jnp.concatenate([w_scr[j] for j in range(n_lane_tiles)], axis=1).astype(BF16)
    acc_t[...] += jnp.dot(vt_ref[...], w_t, preferred_element_type=F32)

    @pl.when(c == pl.num_programs(1) - 1)
    def _():
        o_ref[...] = acc_t[...].T


def _peer_dense(hb2, u_bf, vt_bf, e1, e2, tau, tb=1024, ec=512):
    n_tok = hb2.shape[0]
    tb = min(tb, n_tok)
    nt = tb // LANES
    slab = lambda rows: pl.BlockSpec((nt, rows, LANES), lambda t, c: (t, 0, 0))
    slab_scr = pltpu.VMEM((nt, ec, LANES), F32)
    return pl.pallas_call(
        functools.partial(_peer_dense_kernel, tb=tb, ec=ec),
        grid=(n_tok // tb, N_EXPERTS // ec),
        in_specs=[pl.BlockSpec((tb, D_MODEL), lambda t, c: (t, 0)),
                  pl.BlockSpec((ec, D_MODEL), lambda t, c: (c, 0)),
                  pl.BlockSpec((D_MODEL, ec), lambda t, c: (0, c)),
                  slab(N_KEYS * PEER_HEADS), slab(N_KEYS * PEER_HEADS), slab(SUBLANES)],
        out_specs=pl.BlockSpec((tb, D_MODEL), lambda t, c: (t, 0)),
        out_shape=jax.ShapeDtypeStruct((n_tok, D_MODEL), F32),
        scratch_shapes=[pltpu.VMEM((D_MODEL, tb), F32), slab_scr, slab_scr],
        compiler_params=_cparams("parallel", "arbitrary"),
        name="peer_dense",
    )(hb2, u_bf, vt_bf, e1, e2, tau)


def _final_kernel(h_ref, f_ref, p_ref, g_ref, b_ref, o_ref):
    pre = ALPHA * h_ref[...] + f_ref[...] + p_ref[...]
    mu = jnp.mean(pre, axis=-1, keepdims=True)
    cen = pre - mu
    var = jnp.mean(cen * cen, axis=-1, keepdims=True)
    o_ref[...] = cen * lax.rsqrt(var + LN_EPS) * g_ref[...] + b_ref[...]


def _final(h2, ffn2, ple2, g, b, tm=512):
    n_tok, D = h2.shape
    tm = min(tm, n_tok)
    tok = pl.BlockSpec((tm, D), lambda t: (t, 0))
    vec = pl.BlockSpec((1, D), lambda t: (0, 0))
    return pl.pallas_call(
        _final_kernel,
        grid=(n_tok // tm,),
        in_specs=[tok, tok, tok, vec, vec],
        out_specs=tok,
        out_shape=jax.ShapeDtypeStruct((n_tok, D), F32),
        compiler_params=_cparams("parallel"),
        name="final_norm",
    )(h2, ffn2, ple2, g, b)


def _ffn_and_norm(h3, hb3, ple3, mt, u_bf, vt_bf, ln2_g, ln2_b):
    B, S, D = h3.shape
    n_tok = B * S
    hb2 = hb3.reshape(n_tok, D)
    e1, e2, tau = _peer_gate(hb2, mt)
    ffn = _peer_dense(hb2, u_bf, vt_bf, e1, e2, tau)
    y = _final(h3.reshape(n_tok, D), ffn, ple3.reshape(n_tok, D), ln2_g, ln2_b)
    return y.reshape(B, S, D)


def kernel(x_prompt, x_sample, cache_k, cache_v, state_ssm_re, state_ssm_im, p_prompt, p_sample,
           w_in, b_in, lam_re, lam_im, log_dt, ssm_b_re, ssm_b_im, ssm_c_re, ssm_c_im, ssm_d,
           w_glu, b_glu, g_att_out, g_ssm_out, w_out, b_out, ln1_g, ln1_b,
           w_pq, sub_keys1, sub_keys2, peer_u, peer_v, w_ple, w_ple_gate, ln2_g, ln2_b):
    assert w_in.shape[0] == DEPTH == 1
    B, S, D = x_prompt.shape
    Bs, Ts, _ = x_sample.shape
    row = lambda t: t.reshape(1, -1)

    w_in_bf = w_in[0].astype(BF16)
    b_in2 = row(b_in[0])
    a, bz, cb = _ssm_prep(lam_re[0], lam_im[0], log_dt[0], ssm_b_re[0], ssm_b_im[0], ssm_c_re[0], ssm_c_im[0])
    d_row = row(ssm_d[0])
    tail_w = (w_glu[0].astype(BF16), row(b_glu[0]), row(g_att_out[0]), row(g_ssm_out[0]),
              w_out[0].astype(BF16), row(b_out[0]), row(ln1_g[0]), row(ln1_b[0]),
              w_ple[0].astype(BF16), w_ple_gate[0].astype(BF16))
    mt = _peer_prep(w_pq[0], sub_keys1[0], sub_keys2[0])
    u_bf = peer_u[0].astype(BF16)
    vt_bf = peer_v[0].astype(BF16).T
    g2, b2 = row(ln2_g[0]), row(ln2_b[0])

    q, k_p, v_p, u_tm = _in_proj(x_prompt, w_in_bf, b_in2, time_major=True)
    att = _attn_prompt(q, k_p, v_p)
    z_tm, hre_p, him_p = _ssm_prompt(u_tm.reshape(B // SUBLANES, S * SUBLANES, SSM_WIDTH), a, bz, cb, d_row)
    h3, hb3, ple3 = _tail(x_prompt, att, z_tm, p_prompt[0], tail_w, z_time_major=True)
    y_p = _ffn_and_norm(h3, hb3, ple3, mt, u_bf, vt_bf, g2, b2)

    n_s = Bs * Ts
    q_s, k_s, v_s, u_s = _in_proj(x_sample.reshape(1, n_s, D), w_in_bf, b_in2, time_major=False)
    per_seq = lambda t: t.reshape(Bs, Ts, ATT_WIDTH)
    w_buf = cache_k.shape[2]
    att_s = _attn_sample(per_seq(q_s), per_seq(k_s), per_seq(v_s),
                         cache_k[0].reshape(Bs, w_buf, ATT_WIDTH), cache_v[0].reshape(Bs, w_buf, ATT_WIDTH))
    z_s, hre_s, him_s = _ssm_sample(u_s.reshape(n_s, SSM_WIDTH), state_ssm_re[0].reshape(Bs, SSM_LANES),
                                    state_ssm_im[0].reshape(Bs, SSM_LANES), a, bz, cb, d_row, Ts)
    h3s, hb3s, ple3s = _tail(x_sample.reshape(1, n_s, D), att_s.reshape(1, n_s, ATT_WIDTH),
                             z_s.reshape(1, n_s, SSM_WIDTH), p_sample[0].reshape(1, n_s, PLE_DIM), tail_w,
                             z_time_major=False)
    y_s = _ffn_and_norm(h3s, hb3s, ple3s, mt, u_bf, vt_bf, g2, b2).reshape(Bs, Ts, D)

    w_keep = min(BRANCHES[-1][0], S)
    heads = lambda t, b, s: t.reshape(1, b, s, N_HEADS, HEAD_DIM)
    state = lambda t, b: t.reshape(1, b, N_SSM_GROUPS, SSM_STATE)
    return (y_p, y_s,
            heads(k_p[:, S - w_keep:], B, w_keep), heads(v_p[:, S - w_keep:], B, w_keep),
            state(hre_p, B), state(him_p, B),
            heads(k_s, Bs, Ts), heads(v_s, Bs, Ts),
            state(hre_s, Bs), state(him_s, Bs))
```

```python
import functools
import math

import jax
import jax.numpy as jnp
from jax import lax
from jax.experimental import pallas as pl
from jax.experimental.pallas import tpu as pltpu

D_MODEL = 1024
ATT_WIDTH = 512
SSM_WIDTH = 512
HEAD_DIM = 64
N_HEADS = 8
BRANCHES = ((128, 1), (512, 4), (2048, 16))
BLK = 128
SSM_GROUP = 16
N_SSM_GROUPS = 32
SSM_STATE = 64
SSM_LANES = N_SSM_GROUPS * SSM_STATE
PEER_HEADS = 8
N_KEYS = 128
N_EXPERTS = N_KEYS * N_KEYS
PEER_TOPK = 16
D_KEY = 256
PLE_DIM = 256
DEPTH = 1
ALPHA = (2 * DEPTH) ** 0.25
LN_EPS = 1e-5

LANES = 128
SUBLANES = 8
NEG_BIG = -1e30
VMEM_LIMIT = 56 * 1024 * 1024

F32 = jnp.float32
BF16 = jnp.bfloat16

_NT = (((1,), (1,)), ((), ()))


def _cparams(*sem):
    return pltpu.CompilerParams(dimension_semantics=sem, vmem_limit_bytes=VMEM_LIMIT)


def _gelu(x):
    c = math.sqrt(2.0 / math.pi)
    return 0.5 * x * (1.0 + jnp.tanh(c * (x + 0.044715 * (x * x * x))))


def _sigmoid(x):
    return 1.0 / (1.0 + jnp.exp(-x))


def _in_proj_kernel(x_ref, w_ref, b_ref, q_ref, k_ref, v_ref, u_ref):
    x = x_ref[...].astype(BF16)
    proj = jnp.dot(x, w_ref[...], preferred_element_type=F32) + b_ref[...]
    q_ref[...] = proj[:, :ATT_WIDTH] * (HEAD_DIM ** -0.5)
    k_ref[...] = proj[:, ATT_WIDTH:2 * ATT_WIDTH]
    v_ref[...] = proj[:, 2 * ATT_WIDTH:3 * ATT_WIDTH]
    u_ref[...] = proj[:, 3 * ATT_WIDTH:]


def _in_proj(x3, w_bf, b2, time_major, tm=512):
    B, S, D = x3.shape
    n_out = w_bf.shape[1]
    tm = min(tm, S)
    spec = pl.BlockSpec((None, tm, ATT_WIDTH), lambda b, t: (b, t, 0))
    if time_major:
        u_shape = (B // SUBLANES, S, SUBLANES * SSM_WIDTH)
        u_spec = pl.BlockSpec((None, tm, SSM_WIDTH), lambda b, t: (b // SUBLANES, t, b % SUBLANES))
    else:
        u_shape = (B, S, SSM_WIDTH)
        u_spec = spec
    sds = jax.ShapeDtypeStruct((B, S, ATT_WIDTH), F32)
    return pl.pallas_call(
        _in_proj_kernel,
        grid=(B, S // tm),
        in_specs=[pl.BlockSpec((None, tm, D), lambda b, t: (b, t, 0)),
                  pl.BlockSpec((D, n_out), lambda b, t: (0, 0)),
                  pl.BlockSpec((1, n_out), lambda b, t: (0, 0))],
        out_specs=[spec, spec, spec, u_spec],
        out_shape=[sds, sds, sds, jax.ShapeDtypeStruct(u_shape, F32)],
        compiler_params=_cparams("parallel", "parallel"),
        name="in_proj",
    )(x3, w_bf, b2)


def _alibi_slopes():
    return [2.0 ** (-8.0 * (h + 1) / N_HEADS) for h in range(N_HEADS)]


def _prompt_bias_table():
    slopes = jnp.asarray(_alibi_slopes(), F32).reshape(N_HEADS // 2, 1, 2, 1, 1)
    qi = jnp.arange(BLK)[:, None]
    kj = jnp.arange(2 * BLK)[None, :]
    delta = qi - kj + BLK
    tabs = []
    for window, dil in BRANCHES:
        steps = window // dil
        valid = (delta >= 0) & (delta <= steps)
        dist = (dil * delta).astype(F32)
        tabs.append(jnp.where(valid, -slopes * dist, NEG_BIG))
    return jnp.concatenate(tabs, axis=1)


_ATTN_UNROLL = 4


def _attn_prompt_kernel(q_ref, k_ref, v_ref, bias_ref, o_ref, acc_scr, m_scr, l_scr, *, seq):
    lane = lax.broadcasted_iota(jnp.int32, (BLK, LANES), 1)
    is_a = lane < HEAD_DIM

    for br, (_, dil) in enumerate(BRANCHES):
        nb = (seq // dil) // BLK

        def block_step(idx, carry, br=br, dil=dil, nb=nb):
            r = idx // nb
            n = idx % nb
            start = n * (BLK * dil) + r
            pstart = jnp.maximum(n - 1, 0) * (BLK * dil) + r
            if dil == 1:
                rows = pl.ds(pl.multiple_of(start, BLK), BLK)
                prows = pl.ds(pl.multiple_of(pstart, BLK), BLK)
            else:
                rows = pl.ds(start, BLK, stride=dil)
                prows = pl.ds(pstart, BLK, stride=dil)
            qb = q_ref[rows, :]
            kk = jnp.concatenate([k_ref[prows, :], k_ref[rows, :]], axis=0).astype(BF16)
            vv = jnp.concatenate([v_ref[prows, :], v_ref[rows, :]], axis=0).astype(BF16)
            pen = jnp.where(n == 0, NEG_BIG, 0.0).astype(F32)
            pen_tile = jnp.concatenate([jnp.full((BLK, BLK), pen, F32), jnp.zeros((BLK, BLK), F32)], axis=1)
            outs = []
            for hh in range(2):
                qh = jnp.where(is_a if hh == 0 else jnp.logical_not(is_a), qb, 0.0).astype(BF16)
                s = lax.dot_general(qh, kk, _NT, preferred_element_type=F32)
                s = s + bias_ref[br, hh] + pen_tile
                m = jnp.max(s, axis=-1, keepdims=True)
                p = jnp.exp(s - m)
                l = jnp.sum(p, axis=-1, keepdims=True)
                acc = jnp.dot(p.astype(BF16), vv, preferred_element_type=F32)
                outs.append((acc, m, l))
            (acc_a, m_a, l_a), (acc_b, m_b, l_b) = outs
            acc_scr[br, rows, :] = jnp.where(is_a, acc_a, acc_b)
            m_scr[br, rows, :] = jnp.where(is_a, m_a, m_b)
            l_scr[br, rows, :] = jnp.where(is_a, l_a, l_b)
            return carry

        lax.fori_loop(0, dil * nb, block_step, 0, unroll=_ATTN_UNROLL)

    chunk = 256
    for c0 in range(0, seq, chunk):
        sl = pl.ds(c0, chunk)
        m1, m2, m3 = m_scr[0, sl, :], m_scr[1, sl, :], m_scr[2, sl, :]
        mm = jnp.maximum(jnp.maximum(m1, m2), m3)
        w1, w2, w3 = jnp.exp(m1 - mm), jnp.exp(m2 - mm), jnp.exp(m3 - mm)
        num = w1 * acc_scr[0, sl, :] + w2 * acc_scr[1, sl, :] + w3 * acc_scr[2, sl, :]
        den = w1 * l_scr[0, sl, :] + w2 * l_scr[1, sl, :] + w3 * l_scr[2, sl, :]
        o_ref[sl, :] = num / den


def _attn_prompt(q, k, v):
    B, S, _ = q.shape
    assert S == BRANCHES[-1][1] * BLK, "dilated blocks assume SEQ == 16 * 128"
    bias = _prompt_bias_table()
    spec = pl.BlockSpec((None, S, LANES), lambda b, hp: (b, 0, hp))
    scr = pltpu.VMEM((len(BRANCHES), S, LANES), F32)
    return pl.pallas_call(
        functools.partial(_attn_prompt_kernel, seq=S),
        grid=(B, ATT_WIDTH // LANES),
        in_specs=[spec, spec, spec,
                  pl.BlockSpec((None, len(BRANCHES), 2, BLK, 2 * BLK), lambda b, hp: (hp, 0, 0, 0, 0))],
        out_specs=spec,
        out_shape=jax.ShapeDtypeStruct((B, S, ATT_WIDTH), F32),
        scratch_shapes=[scr, scr, scr],
        compiler_params=_cparams("parallel", "parallel"),
        name="attn_prompt",
    )(q, k, v, bias)


_FAR_DIL = BRANCHES[-1][1]
_TAIL_ROWS = max(w for w, _ in BRANCHES[:-1])


def _attn_sample_kernel(q_ref, kn_ref, vn_ref, kf_ref, vf_ref, kt_ref, vt_ref, o_ref, *, w_buf, t_new):
    slopes = _alibi_slopes()
    lane = lax.broadcasted_iota(jnp.int32, (t_new, LANES), 1)
    is_a = lane < HEAD_DIM
    rows2 = 2 * t_new
    n_far, n_tail = kf_ref.shape[0], kt_ref.shape[0]

    def geometry(n_cols, pos_of_col, col_ok=None):
        t = lax.broadcasted_iota(jnp.int32, (rows2, n_cols), 0) % t_new
        col = lax.broadcasted_iota(jnp.int32, (rows2, n_cols), 1)
        delta = w_buf + t - pos_of_col(col)
        ok = delta >= 0
        if col_ok is not None:
            ok = ok & col_ok(col)
        head_b = lax.broadcasted_iota(jnp.int32, (rows2, n_cols), 0) >= t_new
        return delta, ok, head_b

    geo = [geometry(n_far, lambda c: (c // t_new) * _FAR_DIL + c % t_new),
           geometry(n_tail, lambda c: (w_buf - n_tail) + c),
           geometry(rows2, lambda c: w_buf + c, lambda c: c < t_new)]

    for hp in range(ATT_WIDTH // LANES):
        cs = slice(hp * LANES, (hp + 1) * LANES)
        q = q_ref[:, cs]
        q2 = jnp.concatenate([jnp.where(is_a, q, 0.0), jnp.where(is_a, 0.0, q)], axis=0).astype(BF16)
        zpad = jnp.zeros((t_new, LANES), F32)
        keys = [kf_ref[:, cs], kt_ref[:, cs], jnp.concatenate([kn_ref[:, cs], zpad], axis=0).astype(BF16)]
        vals = [vf_ref[:, cs], vt_ref[:, cs], jnp.concatenate([vn_ref[:, cs], zpad], axis=0).astype(BF16)]
        biased = []
        for kseg, (delta, ok, head_b) in zip(keys, geo):
            s = lax.dot_general(q2, kseg, _NT, preferred_element_type=F32)
            slope = jnp.where(head_b, slopes[2 * hp + 1], slopes[2 * hp]).astype(F32)
            sb = s - slope * delta.astype(F32)
            biased.append([jnp.where(ok & (delta <= window) & ((delta & (dil - 1)) == 0), sb, NEG_BIG)
                           for window, dil in BRANCHES])
        mm = None
        for seg in biased:
            for a in seg:
                am = jnp.max(a, axis=-1, keepdims=True)
                mm = am if mm is None else jnp.maximum(mm, am)
        den, acc = None, None
        for seg, vseg in zip(biased, vals):
            p = sum(jnp.exp(a - mm) for a in seg)
            d = jnp.sum(p, axis=-1, keepdims=True)
            o = jnp.dot(p.astype(BF16), vseg, preferred_element_type=F32)
            den = d if den is None else den + d
            acc = o if acc is None else acc + o
        out2 = acc / den
        o_ref[:, cs] = jnp.where(is_a, out2[:t_new], out2[t_new:])


def _window_segments(cache, t_new):
    B, W = cache.shape[:2]
    head = W - _TAIL_ROWS
    assert W % _FAR_DIL == 0 and head % _FAR_DIL == 0 and t_new <= _FAR_DIL
    far = cache[:, :head].reshape(B, head // _FAR_DIL, _FAR_DIL, ATT_WIDTH)[:, :, :t_new]
    far = far.reshape(B, (head // _FAR_DIL) * t_new, ATT_WIDTH).astype(BF16)
    tail = cache[:, head:].reshape(B, _TAIL_ROWS, ATT_WIDTH).astype(BF16)
    return far, tail


def _attn_sample(q, kn, vn, cache_k, cache_v):
    B, T, _ = q.shape
    W = cache_k.shape[1]
    assert W >= BRANCHES[-1][0], "every dilated key of a sample query must lie inside the window buffer"
    kf, kt = _window_segments(cache_k, T)
    vf, vt = _window_segments(cache_v, T)
    new = pl.BlockSpec((None, T, ATT_WIDTH), lambda b: (b, 0, 0))
    seg = lambda a: pl.BlockSpec((None,) + a.shape[1:], lambda b: (b, 0, 0))
    return pl.pallas_call(
        functools.partial(_attn_sample_kernel, w_buf=W, t_new=T),
        grid=(B,),
        in_specs=[new, new, new, seg(kf), seg(vf), seg(kt), seg(vt)],
        out_specs=new,
        out_shape=jax.ShapeDtypeStruct((B, T, ATT_WIDTH), F32),
        compiler_params=_cparams("parallel"),
        name="attn_sample",
    )(q, kn, vn, kf, vf, kt, vt)


def _ssm_prep_kernel(lr_ref, li_ref, ldt_ref, btr_ref, bti_ref, ctr_ref, cti_ref, a_ref, bz_ref, cb_ref):
    lr, li = lr_ref[...], li_ref[...]
    dt = jnp.exp(ldt_ref[...])
    mag = jnp.exp(lr * dt)
    a_re = mag * jnp.cos(li * dt)
    a_im = mag * jnp.sin(li * dt)
    den = lr * lr + li * li
    nr = a_re - 1.0
    z_re = (nr * lr + a_im * li) / den
    z_im = (a_im * lr - nr * li) / den
    a_ref[0] = jnp.broadcast_to(a_re, (SUBLANES, SSM_LANES))
    a_ref[1] = jnp.broadcast_to(a_im, (SUBLANES, SSM_LANES))
    btr, bti = btr_ref[...], bti_ref[...]
    bz_re = z_re * btr - z_im * bti
    bz_im = z_re * bti + z_im * btr
    row_g = lax.broadcasted_iota(jnp.int32, (SSM_WIDTH, SSM_LANES), 0) // SSM_GROUP
    col_g = lax.broadcasted_iota(jnp.int32, (SSM_WIDTH, SSM_LANES), 1) // SSM_STATE
    diag = row_g == col_g

    def block_diag(small):
        tiled = jnp.concatenate([small] * N_SSM_GROUPS, axis=0)
        return jnp.where(diag, tiled, 0.0).astype(BF16)

    bz_ref[0] = block_diag(bz_re)
    bz_ref[1] = block_diag(bz_im)
    cb_ref[0] = block_diag(ctr_ref[...])
    cb_ref[1] = block_diag(cti_ref[...])


def _ssm_prep(lam_re, lam_im, log_dt, b_re, b_im, c_re, c_im):
    row = lambda t: t.reshape(1, SSM_LANES)
    chan = lambda t: t.reshape(SSM_GROUP, SSM_LANES)
    args = (row(lam_re), row(lam_im), row(jnp.repeat(log_dt, SSM_STATE)),
            chan(b_re.transpose(2, 0, 1)), chan(b_im.transpose(2, 0, 1)),
            chan(c_re.transpose(1, 0, 2)), chan(c_im.transpose(1, 0, 2)))
    return pl.pallas_call(
        _ssm_prep_kernel,
        out_shape=[jax.ShapeDtypeStruct((2, SUBLANES, SSM_LANES), F32),
                   jax.ShapeDtypeStruct((2, SSM_WIDTH, SSM_LANES), BF16),
                   jax.ShapeDtypeStruct((2, SSM_WIDTH, SSM_LANES), BF16)],
        compiler_params=pltpu.CompilerParams(vmem_limit_bytes=VMEM_LIMIT),
        name="ssm_prep",
    )(*args)


_SSM_LANE_CHUNK = 512


def _ssm_prompt_kernel(u_ref, a_ref, bz_ref, cb_ref, d_ref, z_ref, hre_ref, him_ref, xr, xi, hst, *, tc):
    @pl.when(pl.program_id(1) == 0)
    def _():
        hst[...] = jnp.zeros_like(hst)

    u = u_ref[...]
    ub = u.astype(BF16)
    xr[...] = jnp.dot(ub, bz_ref[0], preferred_element_type=F32)
    xi[...] = jnp.dot(ub, bz_ref[1], preferred_element_type=F32)
    for lc in range(SSM_LANES // _SSM_LANE_CHUNK):
        sl = slice(lc * _SSM_LANE_CHUNK, (lc + 1) * _SSM_LANE_CHUNK)
        ar, ai = a_ref[0, :, sl], a_ref[1, :, sl]

        def step(t, carry, sl=sl, ar=ar, ai=ai):
            hr, hi = carry
            rows = pl.ds(pl.multiple_of(t * SUBLANES, SUBLANES), SUBLANES)
            nhr = ar * hr - ai * hi + xr[rows, sl]
            nhi = ar * hi + ai * hr + xi[rows, sl]
            xr[rows, sl] = nhr
            xi[rows, sl] = nhi
            return nhr, nhi

        hr, hi = lax.fori_loop(0, tc, step, (hst[0, :, sl], hst[1, :, sl]))
        hst[0, :, sl] = hr
        hst[1, :, sl] = hi
    y = (lax.dot_general(xr[...].astype(BF16), cb_ref[0], _NT, preferred_element_type=F32)
         - lax.dot_general(xi[...].astype(BF16), cb_ref[1], _NT, preferred_element_type=F32)
         + d_ref[...] * u)
    z_ref[...] = _gelu(y)
    hre_ref[...] = hst[0]
    him_ref[...] = hst[1]


def _ssm_prompt(u_tm, a, bz, cb, d_row, tc=64):
    G, rows_total, _ = u_tm.shape
    S = rows_total // SUBLANES
    rows = tc * SUBLANES
    full3 = lambda shp: pl.BlockSpec(shp, lambda g, t: (0, 0, 0))
    st_spec = pl.BlockSpec((SUBLANES, SSM_LANES), lambda g, t: (g, 0))
    st_sds = jax.ShapeDtypeStruct((G * SUBLANES, SSM_LANES), F32)
    return pl.pallas_call(
        functools.partial(_ssm_prompt_kernel, tc=tc),
        grid=(G, S // tc),
        in_specs=[pl.BlockSpec((None, rows, SSM_WIDTH), lambda g, t: (g, t, 0)),
                  full3((2, SUBLANES, SSM_LANES)), full3((2, SSM_WIDTH, SSM_LANES)),
                  full3((2, SSM_WIDTH, SSM_LANES)),
                  pl.BlockSpec((1, SSM_WIDTH), lambda g, t: (0, 0))],
        out_specs=[pl.BlockSpec((None, rows, SSM_WIDTH), lambda g, t: (g, t, 0)), st_spec, st_spec],
        out_shape=[jax.ShapeDtypeStruct(u_tm.shape, F32), st_sds, st_sds],
        scratch_shapes=[pltpu.VMEM((rows, SSM_LANES), F32), pltpu.VMEM((rows, SSM_LANES), F32),
                        pltpu.VMEM((2, SUBLANES, SSM_LANES), F32)],
        compiler_params=_cparams("parallel", "arbitrary"),
        name="ssm_prompt",
    )(u_tm, a, bz, cb, d_row)


def _ssm_sample_kernel(u_ref, h0r_ref, h0i_ref, a_ref, bz_ref, cb_ref, d_ref, z_ref, hre_ref, him_ref,
                       xr, xi, *, t_new):
    u = u_ref[...]
    ub = u.astype(BF16)
    nb = h0r_ref.shape[0]
    n_tiles = SSM_LANES // LANES
    x_re = jnp.dot(ub, bz_ref[0], preferred_element_type=F32)
    x_im = jnp.dot(ub, bz_ref[1], preferred_element_type=F32)
    for j in range(n_tiles):
        xr[j] = x_re[:, j * LANES:(j + 1) * LANES]
        xi[j] = x_im[:, j * LANES:(j + 1) * LANES]
    for j in range(n_tiles):
        cs = slice(j * LANES, (j + 1) * LANES)
        ar = jnp.broadcast_to(a_ref[0, 0:1, cs], (nb, LANES))
        ai = jnp.broadcast_to(a_ref[1, 0:1, cs], (nb, LANES))
        hr, hi = h0r_ref[:, cs], h0i_ref[:, cs]
        for t in range(t_new):
            rows = pl.ds(t, nb, stride=t_new)
            nhr = ar * hr - ai * hi + xr[j, rows, :]
            nhi = ar * hi + ai * hr + xi[j, rows, :]
            xr[j, rows, :] = nhr
            xi[j, rows, :] = nhi
            hr, hi = nhr, nhi
        hre_ref[:, cs] = hr
        him_ref[:, cs] = hi
    h_re = jnp.concatenate([xr[j] for j in range(n_tiles)], axis=1).astype(BF16)
    h_im = jnp.concatenate([xi[j] for j in range(n_tiles)], axis=1).astype(BF16)
    y = (lax.dot_general(h_re, cb_ref[0], _NT, preferred_element_type=F32)
         - lax.dot_general(h_im, cb_ref[1], _NT, preferred_element_type=F32)
         + d_ref[...] * u)
    z_ref[...] = _gelu(y)


def _ssm_sample(u, h0r, h0i, a, bz, cb, d_row, t_new):
    n_tok = u.shape[0]
    nb = h0r.shape[0]
    slab = pltpu.VMEM((SSM_LANES // LANES, n_tok, LANES), F32)
    st = jax.ShapeDtypeStruct((nb, SSM_LANES), F32)
    return pl.pallas_call(
        functools.partial(_ssm_sample_kernel, t_new=t_new),
        out_shape=[jax.ShapeDtypeStruct((n_tok, SSM_WIDTH), F32), st, st],
        scratch_shapes=[slab, slab],
        compiler_params=pltpu.CompilerParams(vmem_limit_bytes=VMEM_LIMIT),
        name="ssm_sample",
    )(u, h0r, h0i, a, bz, cb, d_row)


def _tail_kernel(x_ref, att_ref, z_ref, p_ref, wglu_ref, bglu_ref, gatt_ref, gssm_ref, wout_ref, bout_ref,
                 g1_ref, b1_ref, wple_ref, wgate_ref, h_ref, ht_ref, ple_ref):
    z = z_ref[...]
    gate = jnp.dot(z.astype(BF16), wglu_ref[...], preferred_element_type=F32) + bglu_ref[...]
    ssm_out = z * _sigmoid(gate)
    att = att_ref[...]
    rms_a = att * lax.rsqrt(jnp.mean(att * att, axis=-1, keepdims=True) + LN_EPS) * gatt_ref[...]
    rms_s = ssm_out * lax.rsqrt(jnp.mean(ssm_out * ssm_out, axis=-1, keepdims=True) + LN_EPS) * gssm_ref[...]
    mix = (jnp.dot(rms_a.astype(BF16), wout_ref[:ATT_WIDTH, :], preferred_element_type=F32)
           + jnp.dot(rms_s.astype(BF16), wout_ref[ATT_WIDTH:, :], preferred_element_type=F32)
           + bout_ref[...])
    pre = ALPHA * x_ref[...] + mix
    mu = jnp.mean(pre, axis=-1, keepdims=True)
    cen = pre - mu
    var = jnp.mean(cen * cen, axis=-1, keepdims=True)
    h = cen * lax.rsqrt(var + LN_EPS) * g1_ref[...] + b1_ref[...]
    hb = h.astype(BF16)
    h_ref[...] = h
    ht_ref[...] = h.T.astype(BF16)
    gate2 = _sigmoid(jnp.dot(hb, wgate_ref[...], preferred_element_type=F32))
    ple_ref[...] = gate2 * jnp.dot(p_ref[...].astype(BF16), wple_ref[...], preferred_element_type=F32)


def _tail(x3, att3, z_arr, p3, weights, z_time_major, tm=256):
    B, S, D = x3.shape
    tm = min(tm, S)
    tok = lambda w: pl.BlockSpec((None, tm, w), lambda b, t: (b, t, 0))
    if z_time_major:
        z_arr = z_arr.reshape(B // SUBLANES, S, SUBLANES * SSM_WIDTH)
        z_spec = pl.BlockSpec((None, tm, SSM_WIDTH), lambda b, t: (b // SUBLANES, t, b % SUBLANES))
    else:
        z_spec = tok(SSM_WIDTH)
    full = lambda a: pl.BlockSpec(a.shape, lambda b, t: (0, 0))
    return pl.pallas_call(
        _tail_kernel,
        grid=(B, S // tm),
        in_specs=[tok(D), tok(ATT_WIDTH), z_spec, tok(PLE_DIM)] + [full(w) for w in weights],
        out_specs=[tok(D), pl.BlockSpec((D, tm), lambda b, t: (0, b * (S // tm) + t)), tok(D)],
        out_shape=[jax.ShapeDtypeStruct((B, S, D), F32), jax.ShapeDtypeStruct((D, B * S), BF16),
                   jax.ShapeDtypeStruct((B, S, D), F32)],
        compiler_params=_cparams("parallel", "parallel"),
        name="tail",
    )(x3, att3, z_arr, p3, *weights)


def _split_bf16(x):
    hi = x.astype(BF16)
    lo = (x - hi.astype(F32)).astype(BF16)
    return hi, lo


def _peer_prep_kernel(wpq_ref, sk1_ref, sk2_ref, m1_ref, m2_ref):
    def combined(sk, w):
        sk_hi, sk_lo = _split_bf16(sk)
        w_hi, w_lo = _split_bf16(w)
        dot = lambda a, b: lax.dot_general(a, b, _NT, preferred_element_type=F32)
        return dot(sk_hi, w_hi) + (dot(sk_hi, w_lo) + dot(sk_lo, w_hi))

    half = D_KEY // 2
    m1_ref[...] = combined(sk1_ref[...], wpq_ref[:, :half]).astype(BF16)
    m2_ref[...] = combined(sk2_ref[...], wpq_ref[:, half:]).astype(BF16)


def _peer_prep(w_pq, sk1, sk2):
    half = D_KEY // 2
    m1, m2 = pl.pallas_call(
        _peer_prep_kernel,
        grid=(PEER_HEADS,),
        in_specs=[pl.BlockSpec((D_MODEL, D_KEY), lambda h: (0, h)),
                  pl.BlockSpec((N_KEYS, half), lambda h: (0, 0)),
                  pl.BlockSpec((N_KEYS, half), lambda h: (0, 0))],
        out_specs=[pl.BlockSpec((N_KEYS, D_MODEL), lambda h: (0, h)),
                   pl.BlockSpec((None, N_KEYS, D_MODEL), lambda h: (h, 0, 0))],
        out_shape=[jax.ShapeDtypeStruct((N_KEYS, PEER_HEADS * D_MODEL), BF16),
                   jax.ShapeDtypeStruct((PEER_HEADS, N_KEYS, D_MODEL), BF16)],
        compiler_params=_cparams("parallel"),
        name="peer_prep",
    )(w_pq, sk1, sk2)
    return jnp.concatenate([m1.reshape(N_KEYS * PEER_HEADS, D_MODEL),
                            m2.reshape(PEER_HEADS * N_KEYS, D_MODEL)], axis=0)


def _vmax(a, b):
    return b if a is None else a if b is None else jnp.maximum(a, b)


def _vmin(a, b):
    return None if a is None or b is None else jnp.minimum(a, b)


def _larger_first(xs, i, l):
    a, b = xs[i], xs[l]
    xs[i], xs[l] = _vmax(a, b), _vmin(a, b)


def _sort_desc(xs):
    xs = list(xs)
    n, k = len(xs), 2
    while k <= n:
        j = k // 2
        while j >= 1:
            for i in range(n):
                l = i ^ j
                if l > i:
                    if (i & k) == 0:
                        _larger_first(xs, i, l)
                    else:
                        _larger_first(xs, l, i)
            j //= 2
        k *= 2
    return xs


def _merge_top(a, b):
    n = len(a)
    c = [_vmax(a[i], b[n - 1 - i]) for i in range(n)]
    j = n // 2
    while j >= 1:
        for i in range(n):
            l = i ^ j
            if l > i:
                _larger_first(c, i, l)
        j //= 2
    return c


def _top_of_lists(lists):
    lists = [l + [None] * (PEER_TOPK - len(l)) for l in lists]
    while len(lists) > 1:
        nxt = [_merge_top(lists[i], lists[i + 1]) for i in range(0, len(lists) - 1, 2)]
        if len(lists) % 2:
            nxt.append(lists[-1])
        lists = nxt
    return lists[0]


_CAND_PAIRS = [(a, b) for a in range(PEER_TOPK) for b in range(PEER_TOPK) if (a + 1) * (b + 1) <= PEER_TOPK]
_CAND_SOLO_ROWS = 5


def _peer_gate_kernel(ht_ref, mt_ref, e1_ref, e2_ref, tau_ref, *, tn):
    st = jnp.dot(mt_ref[...], ht_ref[...], preferred_element_type=F32)
    n1 = N_KEYS * PEER_HEADS
    rowid = lax.broadcasted_iota(jnp.int32, (SUBLANES, LANES), 0)
    for jt in range(tn // LANES):
        cs = slice(jt * LANES, (jt + 1) * LANES)
        s1 = [st[i * PEER_HEADS:(i + 1) * PEER_HEADS, cs] for i in range(N_KEYS)]
        top1 = _top_of_lists([_sort_desc(s1[g:g + PEER_TOPK]) for g in range(0, N_KEYS, PEER_TOPK)])
        max1 = top1[0]
        for i in range(N_KEYS):
            e1_ref[jt, i * PEER_HEADS:(i + 1) * PEER_HEADS, :] = jnp.exp(s1[i] - max1)
        v1 = [jnp.exp(t - max1) for t in top1]
        e2_heads, v2_heads = [], []
        for h in range(PEER_HEADS):
            s2 = st[n1 + h * N_KEYS:n1 + (h + 1) * N_KEYS, cs]
            top = _sort_desc([s2[r * SUBLANES:(r + 1) * SUBLANES] for r in range(N_KEYS // SUBLANES)])
            for shift in (4, 2, 1):
                top = _merge_top(top, [pltpu.roll(t, shift, 0) for t in top])
            max2 = top[0]
            e2_heads.append(jnp.exp(s2 - max2[0:1, :]))
            v2_heads.append([jnp.exp(t - max2) for t in top])
        v2 = []
        for b in range(PEER_TOPK):
            acc = v2_heads[0][b]
            for h in range(1, PEER_HEADS):
                acc = jnp.where(rowid == h, v2_heads[h][b], acc)
            v2.append(acc)
        cand0 = {(a, b): v1[a] * v2[b] for a, b in _CAND_PAIRS}
        lists = [[cand0[(a, b)] for b in range(PEER_TOPK) if (a, b) in cand0] for a in range(_CAND_SOLO_ROWS)]
        lists += [[cand0[(a, b)] for a in range(_CAND_SOLO_ROWS, PEER_TOPK) if (a, b) in cand0]
                  for b in range(PEER_TOPK // (_CAND_SOLO_ROWS + 1))]
        assert sum(len(l) for l in lists) == len(_CAND_PAIRS)
        top = _top_of_lists(lists)
        tau0 = top[PEER_TOPK - 1]
        rz = 1.0 / functools.reduce(lambda x, y: x + y, top)
        v2n = [v * rz for v in v2]
        tau = None
        for (a, b), c0 in cand0.items():
            c = jnp.where(c0 >= tau0, v1[a] * v2n[b], jnp.inf)
            tau = c if tau is None else jnp.minimum(tau, c)
        tau_ref[jt] = tau
        for h in range(PEER_HEADS):
            e2_ref[jt, h * N_KEYS:(h + 1) * N_KEYS, :] = e2_heads[h] * rz[h:h + 1, :]


def _peer_gate(ht, mt, tn=256):
    n_tok = ht.shape[1]
    tn = min(tn, n_tok)
    nt = tn // LANES
    slab = lambda rows: pl.BlockSpec((nt, rows, LANES), lambda t: (t, 0, 0))
    return pl.pallas_call(
        functools.partial(_peer_gate_kernel, tn=tn),
        grid=(n_tok // tn,),
        in_specs=[pl.BlockSpec((D_MODEL, tn), lambda t: (0, t)),
                  pl.BlockSpec(mt.shape, lambda t: (0, 0))],
        out_specs=[slab(N_KEYS * PEER_HEADS), slab(N_KEYS * PEER_HEADS), slab(SUBLANES)],
        out_shape=[jax.ShapeDtypeStruct((n_tok // LANES, N_KEYS * PEER_HEADS, LANES), F32),
                   jax.ShapeDtypeStruct((n_tok // LANES, N_KEYS * PEER_HEADS, LANES), F32),
                   jax.ShapeDtypeStruct((n_tok // LANES, SUBLANES, LANES), F32)],
        compiler_params=_cparams("parallel"),
        name="peer_gate",
    )(ht, mt)


_MXU_COLS = 2 * LANES


def _gelu_times(x, g):
    c = math.sqrt(2.0 / math.pi)
    hx = 0.5 * x
    t = jnp.tanh(x * (x * x * (c * 0.044715) + c))
    return g * (hx + hx * t)


def _peer_dense_kernel(ht_ref, u_ref, vt_ref, e1_ref, e2_ref, tau_ref, o_ref, acc_t, w_even, w_odd, *,
                       tb, ec, n_chunks):
    s = pl.program_id(1)
    n_pairs = tb // _MXU_COLS
    keys_per_chunk = ec // N_KEYS
    chunk = jnp.minimum(s, n_chunks - 1)

    @pl.when(s == 0)
    def _():
        acc_t[...] = jnp.zeros_like(acc_t)
        w_odd[...] = jnp.zeros_like(w_odd)

    def step(w_in, w_out):
        def matmuls(pr):
            acc_t[pr] += jnp.dot(vt_ref[...], w_in[pr], preferred_element_type=F32)
            return jnp.dot(u_ref[...], ht_ref[:, pr * _MXU_COLS:(pr + 1) * _MXU_COLS],
                           preferred_element_type=F32)

        a_next = matmuls(0)
        for pr in range(n_pairs):
            a = a_next
            if pr + 1 < n_pairs:
                a_next = matmuls(pr + 1)
            halves = []
            for half in range(_MXU_COLS // LANES):
                j = pr * (_MXU_COLS // LANES) + half
                cs = slice(half * LANES, (half + 1) * LANES)
                taub = [jnp.broadcast_to(tau_ref[j, h:h + 1, :], (SUBLANES, LANES)) for h in range(PEER_HEADS)]
                pieces = []
                for k1l in range(keys_per_chunk):
                    k1 = chunk * keys_per_chunk + k1l
                    e1b = [jnp.broadcast_to(e1_ref[j, pl.ds(k1 * PEER_HEADS + h, 1), :], (SUBLANES, LANES))
                           for h in range(PEER_HEADS)]
                    for kb in range(N_KEYS // SUBLANES):
                        gate = None
                        for h in range(PEER_HEADS):
                            r2 = h * N_KEYS + kb * SUBLANES
                            prod = e1b[h] * e2_ref[j, r2:r2 + SUBLANES, :]
                            sel = jnp.where(prod >= taub[h], prod, 0.0)
                            gate = sel if gate is None else gate + sel
                        r0 = k1l * N_KEYS + kb * SUBLANES
                        pieces.append(_gelu_times(a[r0:r0 + SUBLANES, cs], gate))
                halves.append(jnp.concatenate(pieces, axis=0))
            w_out[pr] = jnp.concatenate(halves, axis=1).astype(BF16)

    @pl.when(s % 2 == 0)
    def _():
        step(w_odd, w_even)

    @pl.when(s % 2 == 1)
    def _():
        step(w_even, w_odd)

    @pl.when(s == n_chunks)
    def _():
        for pr in range(n_pairs):
            o_ref[pr * _MXU_COLS:(pr + 1) * _MXU_COLS, :] = acc_t[pr].T


def _peer_dense(ht, u_bf, vt_bf, e1, e2, tau, tb=1024, ec=512):
    n_tok = ht.shape[1]
    tb = min(tb, n_tok)
    nt = tb // LANES
    n_chunks = N_EXPERTS // ec
    slab = lambda rows: pl.BlockSpec((nt, rows, LANES), lambda t, s: (t, 0, 0))
    w_buf = pltpu.VMEM((tb // _MXU_COLS, ec, _MXU_COLS), BF16)
    return pl.pallas_call(
        functools.partial(_peer_dense_kernel, tb=tb, ec=ec, n_chunks=n_chunks),
        grid=(n_tok // tb, n_chunks + 1),
        in_specs=[pl.BlockSpec((D_MODEL, tb), lambda t, s: (0, t)),
                  pl.BlockSpec((ec, D_MODEL), lambda t, s: (jnp.minimum(s, n_chunks - 1), 0)),
                  pl.BlockSpec((D_MODEL, ec), lambda t, s: (0, jnp.maximum(s - 1, 0))),
                  slab(N_KEYS * PEER_HEADS), slab(N_KEYS * PEER_HEADS), slab(SUBLANES)],
        out_specs=pl.BlockSpec((tb, D_MODEL), lambda t, s: (t, 0)),
        out_shape=jax.ShapeDtypeStruct((n_tok, D_MODEL), F32),
        scratch_shapes=[pltpu.VMEM((tb // _MXU_COLS, D_MODEL, _MXU_COLS), F32), w_buf, w_buf],
        compiler_params=_cparams("parallel", "arbitrary"),
        name="peer_dense",
    )(ht, u_bf, vt_bf, e1, e2, tau)


def _final_kernel(h_ref, f_ref, p_ref, g_ref, b_ref, o_ref):
    pre = ALPHA * h_ref[...] + f_ref[...] + p_ref[...]
    mu = jnp.mean(pre, axis=-1, keepdims=True)
    cen = pre - mu
    var = jnp.mean(cen * cen, axis=-1, keepdims=True)
    o_ref[...] = cen * lax.rsqrt(var + LN_EPS) * g_ref[...] + b_ref[...]


def _final(h2, ffn2, ple2, g, b, tm=512):
    n_tok, D = h2.shape
    tm = min(tm, n_tok)
    tok = pl.BlockSpec((tm, D), lambda t: (t, 0))
    vec = pl.BlockSpec((1, D), lambda t: (0, 0))
    return pl.pallas_call(
        _final_kernel,
        grid=(n_tok // tm,),
        in_specs=[tok, tok, tok, vec, vec],
        out_specs=tok,
        out_shape=jax.ShapeDtypeStruct((n_tok, D), F32),
        compiler_params=_cparams("parallel"),
        name="final_norm",
    )(h2, ffn2, ple2, g, b)


def _ffn_and_norm(h3, ht, ple3, mt, u_bf, vt_bf, ln2_g, ln2_b):
    B, S, D = h3.shape
    n_tok = B * S
    e1, e2, tau = _peer_gate(ht, mt)
    ffn = _peer_dense(ht, u_bf, vt_bf, e1, e2, tau)
    y = _final(h3.reshape(n_tok, D), ffn, ple3.reshape(n_tok, D), ln2_g, ln2_b)
    return y.reshape(B, S, D)


def kernel(x_prompt, x_sample, cache_k, cache_v, state_ssm_re, state_ssm_im, p_prompt, p_sample,
           w_in, b_in, lam_re, lam_im, log_dt, ssm_b_re, ssm_b_im, ssm_c_re, ssm_c_im, ssm_d,
           w_glu, b_glu, g_att_out, g_ssm_out, w_out, b_out, ln1_g, ln1_b,
           w_pq, sub_keys1, sub_keys2, peer_u, peer_v, w_ple, w_ple_gate, ln2_g, ln2_b):
    assert w_in.shape[0] == DEPTH == 1
    B, S, D = x_prompt.shape
    Bs, Ts, _ = x_sample.shape
    row = lambda t: t.reshape(1, -1)

    w_in_bf = w_in[0].astype(BF16)
    b_in2 = row(b_in[0])
    a, bz, cb = _ssm_prep(lam_re[0], lam_im[0], log_dt[0], ssm_b_re[0], ssm_b_im[0], ssm_c_re[0], ssm_c_im[0])
    d_row = row(ssm_d[0])
    tail_w = (w_glu[0].astype(BF16), row(b_glu[0]), row(g_att_out[0]), row(g_ssm_out[0]),
              w_out[0].astype(BF16), row(b_out[0]), row(ln1_g[0]), row(ln1_b[0]),
              w_ple[0].astype(BF16), w_ple_gate[0].astype(BF16))
    mt = _peer_prep(w_pq[0], sub_keys1[0], sub_keys2[0])
    u_bf = peer_u[0].astype(BF16)
    vt_bf = peer_v[0].astype(BF16).T
    g2, b2 = row(ln2_g[0]), row(ln2_b[0])

    q, k_p, v_p, u_tm = _in_proj(x_prompt, w_in_bf, b_in2, time_major=True)
    att = _attn_prompt(q, k_p, v_p)
    z_tm, hre_p, him_p = _ssm_prompt(u_tm.reshape(B // SUBLANES, S * SUBLANES, SSM_WIDTH), a, bz, cb, d_row)
    h3, hb3, ple3 = _tail(x_prompt, att, z_tm, p_prompt[0], tail_w, z_time_major=True)
    y_p = _ffn_and_norm(h3, hb3, ple3, mt, u_bf, vt_bf, g2, b2)

    n_s = Bs * Ts
    q_s, k_s, v_s, u_s = _in_proj(x_sample.reshape(1, n_s, D), w_in_bf, b_in2, time_major=False)
    per_seq = lambda t: t.reshape(Bs, Ts, ATT_WIDTH)
    att_s = _attn_sample(per_seq(q_s), per_seq(k_s), per_seq(v_s), cache_k[0], cache_v[0])
    z_s, hre_s, him_s = _ssm_sample(u_s.reshape(n_s, SSM_WIDTH), state_ssm_re[0].reshape(Bs, SSM_LANES),
                                    state_ssm_im[0].reshape(Bs, SSM_LANES), a, bz, cb, d_row, Ts)
    h3s, hb3s, ple3s = _tail(x_sample.reshape(1, n_s, D), att_s.reshape(1, n_s, ATT_WIDTH),
                             z_s.reshape(1, n_s, SSM_WIDTH), p_sample[0].reshape(1, n_s, PLE_DIM), tail_w,
                             z_time_major=False)
    y_s = _ffn_and_norm(h3s, hb3s, ple3s, mt, u_bf, vt_bf, g2, b2).reshape(Bs, Ts, D)

    w_keep = min(BRANCHES[-1][0], S)
    heads = lambda t, b, s: t.reshape(1, b, s, N_HEADS, HEAD_DIM)
    state = lambda t, b: t.reshape(1, b, N_SSM_GROUPS, SSM_STATE)
    return (y_p, y_s,
            heads(k_p[:, S - w_keep:], B, w_keep), heads(v_p[:, S - w_keep:], B, w_keep),
            state(hre_p, B), state(him_p, B),
            heads(k_s, Bs, Ts), heads(v_s, Bs, Ts),
            state(hre_s, Bs), state(him_s, Bs))
```

```python
import functools
import math

import jax
import jax.numpy as jnp
import numpy as np
from jax import lax
from jax.experimental import pallas as pl
from jax.experimental.pallas import tpu as pltpu

D_MODEL = 1024
ATT_WIDTH = 512
SSM_WIDTH = 512
HEAD_DIM = 64
N_HEADS = 8
BRANCHES = ((128, 1), (512, 4), (2048, 16))
BLK = 128
SSM_GROUP = 16
N_SSM_GROUPS = 32
SSM_STATE = 64
SSM_LANES = N_SSM_GROUPS * SSM_STATE
PEER_HEADS = 8
N_KEYS = 128
N_EXPERTS = N_KEYS * N_KEYS
PEER_TOPK = 16
D_KEY = 256
PLE_DIM = 256
DEPTH = 1
ALPHA = (2 * DEPTH) ** 0.25
LN_EPS = 1e-5

LANES = 128
SUBLANES = 8
_MXU_COLS = 2 * LANES
NEG_BIG = -1e30
VMEM_LIMIT = 56 * 1024 * 1024

F32 = jnp.float32
BF16 = jnp.bfloat16

_NT = (((1,), (1,)), ((), ()))


def _cparams(*sem):
    return pltpu.CompilerParams(dimension_semantics=sem, vmem_limit_bytes=VMEM_LIMIT)


def _gelu(x):
    c = math.sqrt(2.0 / math.pi)
    return 0.5 * x * (1.0 + jnp.tanh(c * (x + 0.044715 * (x * x * x))))


def _sigmoid(x):
    return 1.0 / (1.0 + jnp.exp(-x))


def _in_proj_kernel(x_ref, w_ref, b_ref, q_ref, k_ref, v_ref, u_ref):
    x = x_ref[...].astype(BF16)
    proj = jnp.dot(x, w_ref[...], preferred_element_type=F32) + b_ref[...]
    q_ref[...] = proj[:, :ATT_WIDTH] * (HEAD_DIM ** -0.5)
    k_ref[...] = proj[:, ATT_WIDTH:2 * ATT_WIDTH]
    v_ref[...] = proj[:, 2 * ATT_WIDTH:3 * ATT_WIDTH]
    u_ref[...] = proj[:, 3 * ATT_WIDTH:]


def _in_proj(x3, w_bf, b2, time_major, tm=512):
    B, S, D = x3.shape
    n_out = w_bf.shape[1]
    tm = min(tm, S)
    spec = pl.BlockSpec((None, tm, ATT_WIDTH), lambda b, t: (b, t, 0))
    if time_major:
        u_shape = (B // SUBLANES, S, SUBLANES * SSM_WIDTH)
        u_spec = pl.BlockSpec((None, tm, SSM_WIDTH), lambda b, t: (b // SUBLANES, t, b % SUBLANES))
    else:
        u_shape = (B, S, SSM_WIDTH)
        u_spec = spec
    sds = jax.ShapeDtypeStruct((B, S, ATT_WIDTH), F32)
    return pl.pallas_call(
        _in_proj_kernel,
        grid=(B, S // tm),
        in_specs=[pl.BlockSpec((None, tm, D), lambda b, t: (b, t, 0)),
                  pl.BlockSpec((D, n_out), lambda b, t: (0, 0)),
                  pl.BlockSpec((1, n_out), lambda b, t: (0, 0))],
        out_specs=[spec, spec, spec, u_spec],
        out_shape=[sds, sds, sds, jax.ShapeDtypeStruct(u_shape, F32)],
        compiler_params=_cparams("parallel", "parallel"),
        name="in_proj",
    )(x3, w_bf, b2)


def _alibi_slopes():
    return [2.0 ** (-8.0 * (h + 1) / N_HEADS) for h in range(N_HEADS)]


def _prompt_bias_table():
    slopes = jnp.asarray(_alibi_slopes(), F32).reshape(N_HEADS // 2, 1, 2, 1, 1)
    qi = jnp.arange(BLK)[:, None]
    kj = jnp.arange(2 * BLK)[None, :]
    delta = qi - kj + BLK
    tabs = []
    for window, dil in BRANCHES:
        steps = window // dil
        valid = (delta >= 0) & (delta <= steps)
        dist = (dil * delta).astype(F32)
        tabs.append(jnp.where(valid, -slopes * dist, NEG_BIG))
    return jnp.concatenate(tabs, axis=1)


_ATTN_UNROLL = 4


def _attn_prompt_kernel(q_ref, k_ref, v_ref, bias_ref, o_ref, acc_scr, m_scr, l_scr, *, seq):
    lane = lax.broadcasted_iota(jnp.int32, (BLK, LANES), 1)
    is_a = lane < HEAD_DIM

    for br, (_, dil) in enumerate(BRANCHES):
        nb = (seq // dil) // BLK

        def block_step(idx, carry, br=br, dil=dil, nb=nb):
            r = idx // nb
            n = idx % nb
            start = n * (BLK * dil) + r
            pstart = jnp.maximum(n - 1, 0) * (BLK * dil) + r
            if dil == 1:
                rows = pl.ds(pl.multiple_of(start, BLK), BLK)
                prows = pl.ds(pl.multiple_of(pstart, BLK), BLK)
            else:
                rows = pl.ds(start, BLK, stride=dil)
                prows = pl.ds(pstart, BLK, stride=dil)
            qb = q_ref[rows, :]
            kk = jnp.concatenate([k_ref[prows, :], k_ref[rows, :]], axis=0).astype(BF16)
            vv = jnp.concatenate([v_ref[prows, :], v_ref[rows, :]], axis=0).astype(BF16)
            pen = jnp.where(n == 0, NEG_BIG, 0.0).astype(F32)
            pen_tile = jnp.concatenate([jnp.full((BLK, BLK), pen, F32), jnp.zeros((BLK, BLK), F32)], axis=1)
            outs = []
            for hh in range(2):
                qh = jnp.where(is_a if hh == 0 else jnp.logical_not(is_a), qb, 0.0).astype(BF16)
                s = lax.dot_general(qh, kk, _NT, preferred_element_type=F32)
                s = s + bias_ref[br, hh] + pen_tile
                m = jnp.max(s, axis=-1, keepdims=True)
                p = jnp.exp(s - m)
                l = jnp.sum(p, axis=-1, keepdims=True)
                acc = jnp.dot(p.astype(BF16), vv, preferred_element_type=F32)
                outs.append((acc, m, l))
            (acc_a, m_a, l_a), (acc_b, m_b, l_b) = outs
            acc_scr[br, rows, :] = jnp.where(is_a, acc_a, acc_b)
            m_scr[br, rows, :] = jnp.where(is_a, m_a, m_b)
            l_scr[br, rows, :] = jnp.where(is_a, l_a, l_b)
            return carry

        lax.fori_loop(0, dil * nb, block_step, 0, unroll=_ATTN_UNROLL)

    chunk = 256
    for c0 in range(0, seq, chunk):
        sl = pl.ds(c0, chunk)
        m1, m2, m3 = m_scr[0, sl, :], m_scr[1, sl, :], m_scr[2, sl, :]
        mm = jnp.maximum(jnp.maximum(m1, m2), m3)
        w1, w2, w3 = jnp.exp(m1 - mm), jnp.exp(m2 - mm), jnp.exp(m3 - mm)
        num = w1 * acc_scr[0, sl, :] + w2 * acc_scr[1, sl, :] + w3 * acc_scr[2, sl, :]
        den = w1 * l_scr[0, sl, :] + w2 * l_scr[1, sl, :] + w3 * l_scr[2, sl, :]
        o_ref[sl, :] = num / den


def _attn_prompt(q, k, v):
    B, S, _ = q.shape
    assert S == BRANCHES[-1][1] * BLK, "dilated blocks assume SEQ == 16 * 128"
    bias = _prompt_bias_table()
    spec = pl.BlockSpec((None, S, LANES), lambda b, hp: (b, 0, hp))
    scr = pltpu.VMEM((len(BRANCHES), S, LANES), F32)
    return pl.pallas_call(
        functools.partial(_attn_prompt_kernel, seq=S),
        grid=(B, ATT_WIDTH // LANES),
        in_specs=[spec, spec, spec,
                  pl.BlockSpec((None, len(BRANCHES), 2, BLK, 2 * BLK), lambda b, hp: (hp, 0, 0, 0, 0))],
        out_specs=spec,
        out_shape=jax.ShapeDtypeStruct((B, S, ATT_WIDTH), F32),
        scratch_shapes=[scr, scr, scr],
        compiler_params=_cparams("parallel", "parallel"),
        name="attn_prompt",
    )(q, k, v, bias)


def _sample_bias_tables(w_buf, t_new):
    slopes = np.asarray(_alibi_slopes(), np.float64)[:, None, None]
    t = np.arange(t_new)[None, :, None]

    def tables(key_pos):
        delta = (w_buf + t) - key_pos[None, None, :]
        out = []
        for window, dil in BRANCHES:
            ok = (delta >= 0) & (delta <= window) & (delta % dil == 0)
            out.append(np.where(ok, -slopes * delta, NEG_BIG))
        return jnp.asarray(np.stack(out).astype(np.float32))

    return tables(np.arange(w_buf)), tables(w_buf + np.arange(t_new))


def _attn_sample_kernel(q_ref, kn_ref, vn_ref, kt_ref, vt_ref, bias_ref, biasn_ref, o_ref):
    outs = []
    for h in range(N_HEADS):
        hs = slice(h * HEAD_DIM, (h + 1) * HEAD_DIM)
        qh = q_ref[:, hs].astype(BF16)
        knh = kn_ref[:, hs].astype(BF16)
        vnh = vn_ref[:, hs].astype(BF16)
        s_c = jnp.dot(qh, kt_ref[h].astype(BF16), preferred_element_type=F32)
        s_n = lax.dot_general(qh, knh, _NT, preferred_element_type=F32)
        sb_c = [s_c + bias_ref[r, h] for r in range(len(BRANCHES))]
        sb_n = [s_n + biasn_ref[r, h] for r in range(len(BRANCHES))]
        mm = None
        for a in sb_c + sb_n:
            am = jnp.max(a, axis=-1, keepdims=True)
            mm = am if mm is None else jnp.maximum(mm, am)
        p_c = functools.reduce(lambda x, y: x + y, [jnp.exp(a - mm) for a in sb_c])
        p_n = functools.reduce(lambda x, y: x + y, [jnp.exp(a - mm) for a in sb_n])
        den = jnp.sum(p_c, axis=-1, keepdims=True) + jnp.sum(p_n, axis=-1, keepdims=True)
        acc = (lax.dot_general(p_c.astype(BF16), vt_ref[h].astype(BF16), _NT, preferred_element_type=F32)
               + jnp.dot(p_n.astype(BF16), vnh, preferred_element_type=F32))
        outs.append(acc / den)
    o_ref[...] = jnp.concatenate(outs, axis=1)


def _attn_sample(q, kn, vn, cache_k, cache_v):
    B, T, _ = q.shape
    W = cache_k.shape[1]
    assert W >= BRANCHES[-1][0], "every dilated key of a sample query must lie inside the window buffer"
    bias, bias_new = _sample_bias_tables(W, T)
    new = pl.BlockSpec((None, T, ATT_WIDTH), lambda b: (b, 0, 0))
    buf = pl.BlockSpec((None, N_HEADS, HEAD_DIM, W), lambda b: (b, 0, 0, 0))
    const = lambda a: pl.BlockSpec(a.shape, lambda b: (0, 0, 0, 0))
    return pl.pallas_call(
        _attn_sample_kernel,
        grid=(B,),
        in_specs=[new, new, new, buf, buf, const(bias), const(bias_new)],
        out_specs=new,
        out_shape=jax.ShapeDtypeStruct((B, T, ATT_WIDTH), F32),
        compiler_params=_cparams("parallel"),
        name="attn_sample",
    )(q, kn, vn, cache_k.transpose(0, 2, 3, 1), cache_v.transpose(0, 2, 3, 1), bias, bias_new)


def _ssm_prep_kernel(lr_ref, li_ref, ldt_ref, btr_ref, bti_ref, ctr_ref, cti_ref, a_ref, bz_ref, cb_ref):
    lr, li = lr_ref[...], li_ref[...]
    dt = jnp.exp(ldt_ref[...])
    mag = jnp.exp(lr * dt)
    a_re = mag * jnp.cos(li * dt)
    a_im = mag * jnp.sin(li * dt)
    den = lr * lr + li * li
    nr = a_re - 1.0
    z_re = (nr * lr + a_im * li) / den
    z_im = (a_im * lr - nr * li) / den
    a_ref[0] = jnp.broadcast_to(a_re, (SUBLANES, SSM_LANES))
    a_ref[1] = jnp.broadcast_to(a_im, (SUBLANES, SSM_LANES))
    btr, bti = btr_ref[...], bti_ref[...]
    bz_re = z_re * btr - z_im * bti
    bz_im = z_re * bti + z_im * btr
    row_g = lax.broadcasted_iota(jnp.int32, (SSM_WIDTH, SSM_LANES), 0) // SSM_GROUP
    col_g = lax.broadcasted_iota(jnp.int32, (SSM_WIDTH, SSM_LANES), 1) // SSM_STATE
    diag = row_g == col_g

    def block_diag(small):
        tiled = jnp.concatenate([small] * N_SSM_GROUPS, axis=0)
        return jnp.where(diag, tiled, 0.0).astype(BF16)

    bz_ref[0] = block_diag(bz_re)
    bz_ref[1] = block_diag(bz_im)
    cb_ref[0] = block_diag(ctr_ref[...])
    cb_ref[1] = block_diag(cti_ref[...])


def _ssm_prep(lam_re, lam_im, log_dt, b_re, b_im, c_re, c_im):
    row = lambda t: t.reshape(1, SSM_LANES)
    chan = lambda t: t.reshape(SSM_GROUP, SSM_LANES)
    args = (row(lam_re), row(lam_im), row(jnp.repeat(log_dt, SSM_STATE)),
            chan(b_re.transpose(2, 0, 1)), chan(b_im.transpose(2, 0, 1)),
            chan(c_re.transpose(1, 0, 2)), chan(c_im.transpose(1, 0, 2)))
    return pl.pallas_call(
        _ssm_prep_kernel,
        out_shape=[jax.ShapeDtypeStruct((2, SUBLANES, SSM_LANES), F32),
                   jax.ShapeDtypeStruct((2, SSM_WIDTH, SSM_LANES), BF16),
                   jax.ShapeDtypeStruct((2, SSM_WIDTH, SSM_LANES), BF16)],
        compiler_params=pltpu.CompilerParams(vmem_limit_bytes=VMEM_LIMIT),
        name="ssm_prep",
    )(*args)


_SSM_LANE_CHUNK = 512


def _ssm_prompt_kernel(u_ref, a_ref, bz_ref, cb_ref, d_ref, z_ref, hre_ref, him_ref, xr, xi, hst, *, tc):
    @pl.when(pl.program_id(1) == 0)
    def _():
        hst[...] = jnp.zeros_like(hst)

    u = u_ref[...]
    ub = u.astype(BF16)
    xr[...] = jnp.dot(ub, bz_ref[0], preferred_element_type=F32)
    xi[...] = jnp.dot(ub, bz_ref[1], preferred_element_type=F32)
    for lc in range(SSM_LANES // _SSM_LANE_CHUNK):
        sl = slice(lc * _SSM_LANE_CHUNK, (lc + 1) * _SSM_LANE_CHUNK)
        ar, ai = a_ref[0, :, sl], a_ref[1, :, sl]

        def step(t, carry, sl=sl, ar=ar, ai=ai):
            hr, hi = carry
            rows = pl.ds(pl.multiple_of(t * SUBLANES, SUBLANES), SUBLANES)
            nhr = ar * hr - ai * hi + xr[rows, sl]
            nhi = ar * hi + ai * hr + xi[rows, sl]
            xr[rows, sl] = nhr
            xi[rows, sl] = nhi
            return nhr, nhi

        hr, hi = lax.fori_loop(0, tc, step, (hst[0, :, sl], hst[1, :, sl]))
        hst[0, :, sl] = hr
        hst[1, :, sl] = hi
    y = (lax.dot_general(xr[...].astype(BF16), cb_ref[0], _NT, preferred_element_type=F32)
         - lax.dot_general(xi[...].astype(BF16), cb_ref[1], _NT, preferred_element_type=F32)
         + d_ref[...] * u)
    z_ref[...] = _gelu(y)
    hre_ref[...] = hst[0]
    him_ref[...] = hst[1]


def _ssm_prompt(u_tm, a, bz, cb, d_row, tc=64):
    G, rows_total, _ = u_tm.shape
    S = rows_total // SUBLANES
    rows = tc * SUBLANES
    full3 = lambda shp: pl.BlockSpec(shp, lambda g, t: (0, 0, 0))
    st_spec = pl.BlockSpec((SUBLANES, SSM_LANES), lambda g, t: (g, 0))
    st_sds = jax.ShapeDtypeStruct((G * SUBLANES, SSM_LANES), F32)
    return pl.pallas_call(
        functools.partial(_ssm_prompt_kernel, tc=tc),
        grid=(G, S // tc),
        in_specs=[pl.BlockSpec((None, rows, SSM_WIDTH), lambda g, t: (g, t, 0)),
                  full3((2, SUBLANES, SSM_LANES)), full3((2, SSM_WIDTH, SSM_LANES)),
                  full3((2, SSM_WIDTH, SSM_LANES)),
                  pl.BlockSpec((1, SSM_WIDTH), lambda g, t: (0, 0))],
        out_specs=[pl.BlockSpec((None, rows, SSM_WIDTH), lambda g, t: (g, t, 0)), st_spec, st_spec],
        out_shape=[jax.ShapeDtypeStruct(u_tm.shape, F32), st_sds, st_sds],
        scratch_shapes=[pltpu.VMEM((rows, SSM_LANES), F32), pltpu.VMEM((rows, SSM_LANES), F32),
                        pltpu.VMEM((2, SUBLANES, SSM_LANES), F32)],
        compiler_params=_cparams("parallel", "arbitrary"),
        name="ssm_prompt",
    )(u_tm, a, bz, cb, d_row)


def _ssm_sample_kernel(u_ref, h0r_ref, h0i_ref, a_ref, bz_ref, cb_ref, d_ref, z_ref, hre_ref, him_ref,
                       xr, xi, *, t_new):
    u = u_ref[...]
    ub = u.astype(BF16)
    nb = h0r_ref.shape[0]
    n_tiles = SSM_LANES // LANES
    x_re = jnp.dot(ub, bz_ref[0], preferred_element_type=F32)
    x_im = jnp.dot(ub, bz_ref[1], preferred_element_type=F32)
    for j in range(n_tiles):
        xr[j] = x_re[:, j * LANES:(j + 1) * LANES]
        xi[j] = x_im[:, j * LANES:(j + 1) * LANES]
    for j in range(n_tiles):
        cs = slice(j * LANES, (j + 1) * LANES)
        ar = jnp.broadcast_to(a_ref[0, 0:1, cs], (nb, LANES))
        ai = jnp.broadcast_to(a_ref[1, 0:1, cs], (nb, LANES))
        hr, hi = h0r_ref[:, cs], h0i_ref[:, cs]
        for t in range(t_new):
            rows = pl.ds(t, nb, stride=t_new)
            nhr = ar * hr - ai * hi + xr[j, rows, :]
            nhi = ar * hi + ai * hr + xi[j, rows, :]
            xr[j, rows, :] = nhr
            xi[j, rows, :] = nhi
            hr, hi = nhr, nhi
        hre_ref[:, cs] = hr
        him_ref[:, cs] = hi
    h_re = jnp.concatenate([xr[j] for j in range(n_tiles)], axis=1).astype(BF16)
    h_im = jnp.concatenate([xi[j] for j in range(n_tiles)], axis=1).astype(BF16)
    y = (lax.dot_general(h_re, cb_ref[0], _NT, preferred_element_type=F32)
         - lax.dot_general(h_im, cb_ref[1], _NT, preferred_element_type=F32)
         + d_ref[...] * u)
    z_ref[...] = _gelu(y)


def _ssm_sample(u, h0r, h0i, a, bz, cb, d_row, t_new):
    n_tok = u.shape[0]
    nb = h0r.shape[0]
    slab = pltpu.VMEM((SSM_LANES // LANES, n_tok, LANES), F32)
    st = jax.ShapeDtypeStruct((nb, SSM_LANES), F32)
    return pl.pallas_call(
        functools.partial(_ssm_sample_kernel, t_new=t_new),
        out_shape=[jax.ShapeDtypeStruct((n_tok, SSM_WIDTH), F32), st, st],
        scratch_shapes=[slab, slab],
        compiler_params=pltpu.CompilerParams(vmem_limit_bytes=VMEM_LIMIT),
        name="ssm_sample",
    )(u, h0r, h0i, a, bz, cb, d_row)


def _tail_kernel(x_ref, att_ref, z_ref, p_ref, wglu_ref, bglu_ref, gatt_ref, gssm_ref, wout_ref, bout_ref,
                 g1_ref, b1_ref, wple_ref, wgate_ref, h_ref, ht_ref, ple_ref):
    z = z_ref[...]
    gate = jnp.dot(z.astype(BF16), wglu_ref[...], preferred_element_type=F32) + bglu_ref[...]
    ssm_out = z * _sigmoid(gate)
    att = att_ref[...]
    rms_a = att * lax.rsqrt(jnp.mean(att * att, axis=-1, keepdims=True) + LN_EPS) * gatt_ref[...]
    rms_s = ssm_out * lax.rsqrt(jnp.mean(ssm_out * ssm_out, axis=-1, keepdims=True) + LN_EPS) * gssm_ref[...]
    mix = (jnp.dot(rms_a.astype(BF16), wout_ref[:ATT_WIDTH, :], preferred_element_type=F32)
           + jnp.dot(rms_s.astype(BF16), wout_ref[ATT_WIDTH:, :], preferred_element_type=F32)
           + bout_ref[...])
    pre = ALPHA * x_ref[...] + mix
    mu = jnp.mean(pre, axis=-1, keepdims=True)
    cen = pre - mu
    var = jnp.mean(cen * cen, axis=-1, keepdims=True)
    h = cen * lax.rsqrt(var + LN_EPS) * g1_ref[...] + b1_ref[...]
    hb = h.astype(BF16)
    h_ref[...] = h
    ht_ref[...] = h.T.astype(BF16)
    gate2 = _sigmoid(jnp.dot(hb, wgate_ref[...], preferred_element_type=F32))
    ple_ref[...] = gate2 * jnp.dot(p_ref[...].astype(BF16), wple_ref[...], preferred_element_type=F32)


def _tail(x3, att3, z_arr, p3, weights, z_time_major):
    B, S, D = x3.shape
    tm = _MXU_COLS
    tok = lambda w: pl.BlockSpec((None, tm, w), lambda b, t: (b, t, 0))
    if z_time_major:
        z_arr = z_arr.reshape(B // SUBLANES, S, SUBLANES * SSM_WIDTH)
        z_spec = pl.BlockSpec((None, tm, SSM_WIDTH), lambda b, t: (b // SUBLANES, t, b % SUBLANES))
    else:
        z_spec = tok(SSM_WIDTH)
    full = lambda a: pl.BlockSpec(a.shape, lambda b, t: (0, 0))
    return pl.pallas_call(
        _tail_kernel,
        grid=(B, S // tm),
        in_specs=[tok(D), tok(ATT_WIDTH), z_spec, tok(PLE_DIM)] + [full(w) for w in weights],
        out_specs=[tok(D), pl.BlockSpec((None, D, tm), lambda b, t: (b * (S // tm) + t, 0, 0)), tok(D)],
        out_shape=[jax.ShapeDtypeStruct((B, S, D), F32), jax.ShapeDtypeStruct((B * S // tm, D, tm), BF16),
                   jax.ShapeDtypeStruct((B, S, D), F32)],
        compiler_params=_cparams("parallel", "parallel"),
        name="tail",
    )(x3, att3, z_arr, p3, *weights)


def _split_bf16(x):
    hi = x.astype(BF16)
    lo = (x - hi.astype(F32)).astype(BF16)
    return hi, lo


def _peer_prep_kernel(wpq_ref, sk1_ref, sk2_ref, m1_ref, m2_ref):
    def combined(sk, w):
        sk_hi, sk_lo = _split_bf16(sk)
        w_hi, w_lo = _split_bf16(w)
        dot = lambda a, b: lax.dot_general(a, b, _NT, preferred_element_type=F32)
        return dot(sk_hi, w_hi) + (dot(sk_hi, w_lo) + dot(sk_lo, w_hi))

    half = D_KEY // 2
    m1_ref[...] = combined(sk1_ref[...], wpq_ref[:, :half]).astype(BF16)
    m2_ref[...] = combined(sk2_ref[...], wpq_ref[:, half:]).astype(BF16)


def _peer_prep(w_pq, sk1, sk2):
    half = D_KEY // 2
    m1, m2 = pl.pallas_call(
        _peer_prep_kernel,
        grid=(PEER_HEADS,),
        in_specs=[pl.BlockSpec((D_MODEL, D_KEY), lambda h: (0, h)),
                  pl.BlockSpec((N_KEYS, half), lambda h: (0, 0)),
                  pl.BlockSpec((N_KEYS, half), lambda h: (0, 0))],
        out_specs=[pl.BlockSpec((N_KEYS, D_MODEL), lambda h: (0, h)),
                   pl.BlockSpec((None, N_KEYS, D_MODEL), lambda h: (h, 0, 0))],
        out_shape=[jax.ShapeDtypeStruct((N_KEYS, PEER_HEADS * D_MODEL), BF16),
                   jax.ShapeDtypeStruct((PEER_HEADS, N_KEYS, D_MODEL), BF16)],
        compiler_params=_cparams("parallel"),
        name="peer_prep",
    )(w_pq, sk1, sk2)
    return jnp.concatenate([m1.reshape(N_KEYS * PEER_HEADS, D_MODEL),
                            m2.reshape(PEER_HEADS * N_KEYS, D_MODEL)], axis=0)


def _vmax(a, b):
    return b if a is None else a if b is None else jnp.maximum(a, b)


def _vmin(a, b):
    return None if a is None or b is None else jnp.minimum(a, b)


def _larger_first(xs, i, l):
    a, b = xs[i], xs[l]
    xs[i], xs[l] = _vmax(a, b), _vmin(a, b)


def _sort_desc(xs):
    xs = list(xs)
    n, k = len(xs), 2
    while k <= n:
        j = k // 2
        while j >= 1:
            for i in range(n):
                l = i ^ j
                if l > i:
                    if (i & k) == 0:
                        _larger_first(xs, i, l)
                    else:
                        _larger_first(xs, l, i)
            j //= 2
        k *= 2
    return xs


def _merge_top(a, b):
    n = len(a)
    c = [_vmax(a[i], b[n - 1 - i]) for i in range(n)]
    j = n // 2
    while j >= 1:
        for i in range(n):
            l = i ^ j
            if l > i:
                _larger_first(c, i, l)
        j //= 2
    return c


def _top_of_lists(lists):
    lists = [l + [None] * (PEER_TOPK - len(l)) for l in lists]
    while len(lists) > 1:
        nxt = [_merge_top(lists[i], lists[i + 1]) for i in range(0, len(lists) - 1, 2)]
        if len(lists) % 2:
            nxt.append(lists[-1])
        lists = nxt
    return lists[0]


_CAND_PAIRS = [(a, b) for a in range(PEER_TOPK) for b in range(PEER_TOPK) if (a + 1) * (b + 1) <= PEER_TOPK]
_CAND_SOLO_ROWS = 5


def _peer_gate_kernel(ht_ref, mt_ref, e1_ref, e2_ref, tau_ref, *, tn):
    st = jnp.dot(mt_ref[...], ht_ref[...], preferred_element_type=F32)
    n1 = N_KEYS * PEER_HEADS
    rowid = lax.broadcasted_iota(jnp.int32, (SUBLANES, LANES), 0)
    for jt in range(tn // LANES):
        cs = slice(jt * LANES, (jt + 1) * LANES)
        s1 = [st[i * PEER_HEADS:(i + 1) * PEER_HEADS, cs] for i in range(N_KEYS)]
        top1 = _top_of_lists([_sort_desc(s1[g:g + PEER_TOPK]) for g in range(0, N_KEYS, PEER_TOPK)])
        max1 = top1[0]
        for i in range(N_KEYS):
            e1_ref[jt, i * PEER_HEADS:(i + 1) * PEER_HEADS, :] = jnp.exp(s1[i] - max1)
        v1 = [jnp.exp(t - max1) for t in top1]
        e2_heads, v2_heads = [], []
        for h in range(PEER_HEADS):
            s2 = st[n1 + h * N_KEYS:n1 + (h + 1) * N_KEYS, cs]
            top = _sort_desc([s2[r * SUBLANES:(r + 1) * SUBLANES] for r in range(N_KEYS // SUBLANES)])
            for shift in (4, 2, 1):
                top = _merge_top(top, [pltpu.roll(t, shift, 0) for t in top])
            max2 = top[0]
            e2_heads.append(jnp.exp(s2 - max2[0:1, :]))
            v2_heads.append([jnp.exp(t - max2) for t in top])
        v2 = []
        for b in range(PEER_TOPK):
            acc = v2_heads[0][b]
            for h in range(1, PEER_HEADS):
                acc = jnp.where(rowid == h, v2_heads[h][b], acc)
            v2.append(acc)
        cand0 = {(a, b): v1[a] * v2[b] for a, b in _CAND_PAIRS}
        lists = [[cand0[(a, b)] for b in range(PEER_TOPK) if (a, b) in cand0] for a in range(_CAND_SOLO_ROWS)]
        lists += [[cand0[(a, b)] for a in range(_CAND_SOLO_ROWS, PEER_TOPK) if (a, b) in cand0]
                  for b in range(PEER_TOPK // (_CAND_SOLO_ROWS + 1))]
        assert sum(len(l) for l in lists) == len(_CAND_PAIRS)
        top = _top_of_lists(lists)
        tau0 = top[PEER_TOPK - 1]
        rz = 1.0 / functools.reduce(lambda x, y: x + y, top)
        v2n = [v * rz for v in v2]
        tau = None
        for (a, b), c0 in cand0.items():
            c = jnp.where(c0 >= tau0, v1[a] * v2n[b], jnp.inf)
            tau = c if tau is None else jnp.minimum(tau, c)
        tau_ref[jt] = tau
        for h in range(PEER_HEADS):
            e2_ref[jt, h * N_KEYS:(h + 1) * N_KEYS, :] = e2_heads[h] * rz[h:h + 1, :]


def _peer_gate(ht, mt):
    tn = _MXU_COLS
    n_tok = ht.shape[0] * tn
    nt = tn // LANES
    slab = lambda rows: pl.BlockSpec((nt, rows, LANES), lambda t: (t, 0, 0))
    return pl.pallas_call(
        functools.partial(_peer_gate_kernel, tn=tn),
        grid=(n_tok // tn,),
        in_specs=[pl.BlockSpec((None, D_MODEL, tn), lambda t: (t, 0, 0)),
                  pl.BlockSpec(mt.shape, lambda t: (0, 0))],
        out_specs=[slab(N_KEYS * PEER_HEADS), slab(N_KEYS * PEER_HEADS), slab(SUBLANES)],
        out_shape=[jax.ShapeDtypeStruct((n_tok // LANES, N_KEYS * PEER_HEADS, LANES), F32),
                   jax.ShapeDtypeStruct((n_tok // LANES, N_KEYS * PEER_HEADS, LANES), F32),
                   jax.ShapeDtypeStruct((n_tok // LANES, SUBLANES, LANES), F32)],
        compiler_params=_cparams("parallel"),
        name="peer_gate",
    )(ht, mt)


def _gelu_times(x, g):
    c = math.sqrt(2.0 / math.pi)
    hx = 0.5 * x
    t = jnp.tanh(x * (x * x * (c * 0.044715) + c))
    return g * (hx + hx * t)


def _peer_dense_kernel(ht_ref, u_ref, vt_ref, e1_ref, e2_ref, tau_ref, o_ref, acc_t, w_even, w_odd, *,
                       tb, ec, n_chunks):
    s = pl.program_id(1)
    n_pairs = tb // _MXU_COLS
    keys_per_chunk = ec // N_KEYS
    n_iter = n_pairs * keys_per_chunk
    v_rows = D_MODEL // keys_per_chunk
    chunk = jnp.minimum(s, n_chunks - 1)

    @pl.when(s == 0)
    def _():
        acc_t[...] = jnp.zeros_like(acc_t)
        w_odd[...] = jnp.zeros_like(w_odd)

    def scores(it):
        pr, k1l = it // keys_per_chunk, it % keys_per_chunk
        return jnp.dot(u_ref[k1l * N_KEYS:(k1l + 1) * N_KEYS, :], ht_ref[pr], preferred_element_type=F32)

    def step(w_in, w_out):
        a_next = scores(0)
        for it in range(n_iter):
            pr, k1l = it // keys_per_chunk, it % keys_per_chunk
            a = a_next
            if it + 1 < n_iter:
                a_next = scores(it + 1)
            vrows = slice(k1l * v_rows, (k1l + 1) * v_rows)
            acc_t[pr, vrows, :] += jnp.dot(vt_ref[vrows, :], w_in[pr], preferred_element_type=F32)
            k1 = chunk * keys_per_chunk + k1l
            halves = []
            for half in range(_MXU_COLS // LANES):
                j = pr * (_MXU_COLS // LANES) + half
                cs = slice(half * LANES, (half + 1) * LANES)
                taub = [jnp.broadcast_to(tau_ref[j, pl.ds(h, 1), :], (SUBLANES, LANES)) for h in range(PEER_HEADS)]
                e1b = [jnp.broadcast_to(e1_ref[j, pl.ds(k1 * PEER_HEADS + h, 1), :], (SUBLANES, LANES))
                       for h in range(PEER_HEADS)]
                pieces = []
                for kb in range(N_KEYS // SUBLANES):
                    gate = None
                    for h in range(PEER_HEADS):
                        prod = e1b[h] * e2_ref[j, pl.ds(h * N_KEYS + kb * SUBLANES, SUBLANES), :]
                        sel = jnp.where(prod >= taub[h], prod, 0.0)
                        gate = sel if gate is None else gate + sel
                    pieces.append(_gelu_times(a[kb * SUBLANES:(kb + 1) * SUBLANES, cs], gate))
                halves.append(jnp.concatenate(pieces, axis=0))
            w_out[pr, k1l * N_KEYS:(k1l + 1) * N_KEYS, :] = jnp.concatenate(halves, axis=1).astype(BF16)

    @pl.when(s % 2 == 0)
    def _():
        step(w_odd, w_even)

    @pl.when(s % 2 == 1)
    def _():
        step(w_even, w_odd)

    @pl.when(s == n_chunks)
    def _():
        for pr in range(n_pairs):
            o_ref[pr * _MXU_COLS:(pr + 1) * _MXU_COLS, :] = acc_t[pr].T


def _peer_dense(ht, u_bf, vt_bf, e1, e2, tau, tb=1024, ec=512):
    n_tok = ht.shape[0] * _MXU_COLS
    tb = min(tb, n_tok)
    nt = tb // LANES
    n_chunks = N_EXPERTS // ec
    slab = lambda rows: pl.BlockSpec((nt, rows, LANES), lambda t, s: (t, 0, 0))
    w_buf = pltpu.VMEM((tb // _MXU_COLS, ec, _MXU_COLS), BF16)
    return pl.pallas_call(
        functools.partial(_peer_dense_kernel, tb=tb, ec=ec, n_chunks=n_chunks),
        grid=(n_tok // tb, n_chunks + 1),
        in_specs=[pl.BlockSpec((tb // _MXU_COLS, D_MODEL, _MXU_COLS), lambda t, s: (t, 0, 0)),
                  pl.BlockSpec((ec, D_MODEL), lambda t, s: (jnp.minimum(s, n_chunks - 1), 0)),
                  pl.BlockSpec((D_MODEL, ec), lambda t, s: (0, jnp.maximum(s - 1, 0))),
                  slab(N_KEYS * PEER_HEADS), slab(N_KEYS * PEER_HEADS), slab(SUBLANES)],
        out_specs=pl.BlockSpec((tb, D_MODEL), lambda t, s: (t, 0)),
        out_shape=jax.ShapeDtypeStruct((n_tok, D_MODEL), F32),
        scratch_shapes=[pltpu.VMEM((tb // _MXU_COLS, D_MODEL, _MXU_COLS), F32), w_buf, w_buf],
        compiler_params=_cparams("parallel", "arbitrary"),
        name="peer_dense",
    )(ht, u_bf, vt_bf, e1, e2, tau)


def _final_kernel(h_ref, f_ref, p_ref, g_ref, b_ref, o_ref):
    pre = ALPHA * h_ref[...] + f_ref[...] + p_ref[...]
    mu = jnp.mean(pre, axis=-1, keepdims=True)
    cen = pre - mu
    var = jnp.mean(cen * cen, axis=-1, keepdims=True)
    o_ref[...] = cen * lax.rsqrt(var + LN_EPS) * g_ref[...] + b_ref[...]


def _final(h2, ffn2, ple2, g, b, tm=512):
    n_tok, D = h2.shape
    tm = min(tm, n_tok)
    tok = pl.BlockSpec((tm, D), lambda t: (t, 0))
    vec = pl.BlockSpec((1, D), lambda t: (0, 0))
    return pl.pallas_call(
        _final_kernel,
        grid=(n_tok // tm,),
        in_specs=[tok, tok, tok, vec, vec],
        out_specs=tok,
        out_shape=jax.ShapeDtypeStruct((n_tok, D), F32),
        compiler_params=_cparams("parallel"),
        name="final_norm",
    )(h2, ffn2, ple2, g, b)


def _ffn_and_norm(h3, ht, ple3, mt, u_bf, vt_bf, ln2_g, ln2_b):
    B, S, D = h3.shape
    n_tok = B * S
    e1, e2, tau = _peer_gate(ht, mt)
    ffn = _peer_dense(ht, u_bf, vt_bf, e1, e2, tau)
    y = _final(h3.reshape(n_tok, D), ffn, ple3.reshape(n_tok, D), ln2_g, ln2_b)
    return y.reshape(B, S, D)


def kernel(x_prompt, x_sample, cache_k, cache_v, state_ssm_re, state_ssm_im, p_prompt, p_sample,
           w_in, b_in, lam_re, lam_im, log_dt, ssm_b_re, ssm_b_im, ssm_c_re, ssm_c_im, ssm_d,
           w_glu, b_glu, g_att_out, g_ssm_out, w_out, b_out, ln1_g, ln1_b,
           w_pq, sub_keys1, sub_keys2, peer_u, peer_v, w_ple, w_ple_gate, ln2_g, ln2_b):
    assert w_in.shape[0] == DEPTH == 1
    B, S, D = x_prompt.shape
    Bs, Ts, _ = x_sample.shape
    row = lambda t: t.reshape(1, -1)

    w_in_bf = w_in[0].astype(BF16)
    b_in2 = row(b_in[0])
    a, bz, cb = _ssm_prep(lam_re[0], lam_im[0], log_dt[0], ssm_b_re[0], ssm_b_im[0], ssm_c_re[0], ssm_c_im[0])
    d_row = row(ssm_d[0])
    tail_w = (w_glu[0].astype(BF16), row(b_glu[0]), row(g_att_out[0]), row(g_ssm_out[0]),
              w_out[0].astype(BF16), row(b_out[0]), row(ln1_g[0]), row(ln1_b[0]),
              w_ple[0].astype(BF16), w_ple_gate[0].astype(BF16))
    mt = _peer_prep(w_pq[0], sub_keys1[0], sub_keys2[0])
    u_bf = peer_u[0].astype(BF16)
    vt_bf = peer_v[0].astype(BF16).T
    g2, b2 = row(ln2_g[0]), row(ln2_b[0])

    q, k_p, v_p, u_tm = _in_proj(x_prompt, w_in_bf, b_in2, time_major=True)
    att = _attn_prompt(q, k_p, v_p)
    z_tm, hre_p, him_p = _ssm_prompt(u_tm.reshape(B // SUBLANES, S * SUBLANES, SSM_WIDTH), a, bz, cb, d_row)
    h3, hb3, ple3 = _tail(x_prompt, att, z_tm, p_prompt[0], tail_w, z_time_major=True)
    y_p = _ffn_and_norm(h3, hb3, ple3, mt, u_bf, vt_bf, g2, b2)

    n_s = Bs * Ts
    q_s, k_s, v_s, u_s = _in_proj(x_sample.reshape(1, n_s, D), w_in_bf, b_in2, time_major=False)
    per_seq = lambda t: t.reshape(Bs, Ts, ATT_WIDTH)
    att_s = _attn_sample(per_seq(q_s), per_seq(k_s), per_seq(v_s), cache_k[0], cache_v[0])
    z_s, hre_s, him_s = _ssm_sample(u_s.reshape(n_s, SSM_WIDTH), state_ssm_re[0].reshape(Bs, SSM_LANES),
                                    state_ssm_im[0].reshape(Bs, SSM_LANES), a, bz, cb, d_row, Ts)
    h3s, hb3s, ple3s = _tail(x_sample.reshape(1, n_s, D), att_s.reshape(1, n_s, ATT_WIDTH),
                             z_s.reshape(1, n_s, SSM_WIDTH), p_sample[0].reshape(1, n_s, PLE_DIM), tail_w,
                             z_time_major=False)
    y_s = _ffn_and_norm(h3s, hb3s, ple3s, mt, u_bf, vt_bf, g2, b2).reshape(Bs, Ts, D)

    w_keep = min(BRANCHES[-1][0], S)
    heads = lambda t, b, s: t.reshape(1, b, s, N_HEADS, HEAD_DIM)
    state = lambda t, b: t.reshape(1, b, N_SSM_GROUPS, SSM_STATE)
    return (y_p, y_s,
            heads(k_p[:, S - w_keep:], B, w_keep), heads(v_p[:, S - w_keep:], B, w_keep),
            state(hre_p, B), state(him_p, B),
            heads(k_s, Bs, Ts), heads(v_s, Bs, Ts),
            state(hre_s, Bs), state(him_s, Bs))
```

```python
import functools
import math

import jax
import jax.numpy as jnp
import numpy as np
from jax import lax
from jax.experimental import pallas as pl
from jax.experimental.pallas import tpu as pltpu

D_MODEL = 1024
ATT_WIDTH = 512
SSM_WIDTH = 512
HEAD_DIM = 64
N_HEADS = 8
BRANCHES = ((128, 1), (512, 4), (2048, 16))
BLK = 128
SSM_GROUP = 16
N_SSM_GROUPS = 32
SSM_STATE = 64
SSM_LANES = N_SSM_GROUPS * SSM_STATE
PEER_HEADS = 8
N_KEYS = 128
N_EXPERTS = N_KEYS * N_KEYS
PEER_TOPK = 16
D_KEY = 256
PLE_DIM = 256
DEPTH = 1
ALPHA = (2 * DEPTH) ** 0.25
LN_EPS = 1e-5

LANES = 128
SUBLANES = 8
_MXU_COLS = 2 * LANES
NEG_BIG = -1e30
VMEM_LIMIT = 56 * 1024 * 1024

F32 = jnp.float32
BF16 = jnp.bfloat16

_NT = (((1,), (1,)), ((), ()))


def _cparams(*sem):
    return pltpu.CompilerParams(dimension_semantics=sem, vmem_limit_bytes=VMEM_LIMIT)


def _gelu(x):
    c = math.sqrt(2.0 / math.pi)
    return 0.5 * x * (1.0 + jnp.tanh(c * (x + 0.044715 * (x * x * x))))


def _sigmoid(x):
    return 1.0 / (1.0 + jnp.exp(-x))


def _in_proj_kernel(x_ref, w_ref, b_ref, q_ref, k_ref, v_ref, u_ref, *kv_t_refs):
    x = x_ref[...].astype(BF16)
    proj = jnp.dot(x, w_ref[...], preferred_element_type=F32) + b_ref[...]
    q_ref[...] = proj[:, :ATT_WIDTH] * (HEAD_DIM ** -0.5)
    k = proj[:, ATT_WIDTH:2 * ATT_WIDTH]
    v = proj[:, 2 * ATT_WIDTH:3 * ATT_WIDTH]
    k_ref[...] = k
    v_ref[...] = v
    u_ref[...] = proj[:, 3 * ATT_WIDTH:]
    if kv_t_refs:
        kv_t_refs[0][...] = k.T
        kv_t_refs[1][...] = v.T


def _in_proj(x3, w_bf, b2, prompt, tm=512):
    B, S, D = x3.shape
    n_out = w_bf.shape[1]
    tm = min(tm, S)
    spec = pl.BlockSpec((None, tm, ATT_WIDTH), lambda b, t: (b, t, 0))
    sds = jax.ShapeDtypeStruct((B, S, ATT_WIDTH), F32)
    out_specs, out_shape = [spec, spec, spec], [sds, sds, sds]
    if prompt:
        out_specs.append(pl.BlockSpec((None, tm, SSM_WIDTH), lambda b, t: (b // SUBLANES, t, b % SUBLANES)))
        out_shape.append(jax.ShapeDtypeStruct((B // SUBLANES, S, SUBLANES * SSM_WIDTH), F32))
        spec_t = pl.BlockSpec((None, ATT_WIDTH, tm), lambda b, t: (b, 0, t))
        out_specs += [spec_t, spec_t]
        out_shape += [jax.ShapeDtypeStruct((B, ATT_WIDTH, S), F32)] * 2
    else:
        out_specs.append(spec)
        out_shape.append(jax.ShapeDtypeStruct((B, S, SSM_WIDTH), F32))
    return pl.pallas_call(
        _in_proj_kernel,
        grid=(B, S // tm),
        in_specs=[pl.BlockSpec((None, tm, D), lambda b, t: (b, t, 0)),
                  pl.BlockSpec((D, n_out), lambda b, t: (0, 0)),
                  pl.BlockSpec((1, n_out), lambda b, t: (0, 0))],
        out_specs=out_specs,
        out_shape=out_shape,
        compiler_params=_cparams("parallel", "parallel"),
        name="in_proj",
    )(x3, w_bf, b2)


def _alibi_slopes():
    return [2.0 ** (-8.0 * (h + 1) / N_HEADS) for h in range(N_HEADS)]


def _prompt_bias_table():
    slopes = jnp.asarray(_alibi_slopes(), F32).reshape(N_HEADS // 2, 1, 2, 1, 1)
    qi = jnp.arange(BLK)[:, None]
    kj = jnp.arange(2 * BLK)[None, :]
    delta = qi - kj + BLK
    tabs = []
    for window, dil in BRANCHES:
        steps = window // dil
        valid = (delta >= 0) & (delta <= steps)
        dist = (dil * delta).astype(F32)
        tabs.append(jnp.where(valid, -slopes * dist, NEG_BIG))
    return jnp.concatenate(tabs, axis=1)


_ATTN_UNROLL = 8


def _attn_prompt_kernel(q_ref, k_ref, v_ref, bias_ref, o_ref, acc_scr, m_scr, l_scr, *, seq):
    lane = lax.broadcasted_iota(jnp.int32, (BLK, LANES), 1)
    is_a = lane < HEAD_DIM

    for br, (_, dil) in enumerate(BRANCHES):
        nb = (seq // dil) // BLK

        def block_step(idx, carry, br=br, dil=dil, nb=nb):
            r = idx // nb
            n = idx % nb
            start = n * (BLK * dil) + r
            pstart = jnp.maximum(n - 1, 0) * (BLK * dil) + r
            if dil == 1:
                rows = pl.ds(pl.multiple_of(start, BLK), BLK)
                prows = pl.ds(pl.multiple_of(pstart, BLK), BLK)
            else:
                rows = pl.ds(start, BLK, stride=dil)
                prows = pl.ds(pstart, BLK, stride=dil)
            qb = q_ref[rows, :]
            kk = jnp.concatenate([k_ref[prows, :], k_ref[rows, :]], axis=0).astype(BF16)
            vv = jnp.concatenate([v_ref[prows, :], v_ref[rows, :]], axis=0).astype(BF16)
            pen = jnp.where(n == 0, NEG_BIG, 0.0).astype(F32)
            pen_tile = jnp.concatenate([jnp.full((BLK, BLK), pen, F32), jnp.zeros((BLK, BLK), F32)], axis=1)
            outs = []
            for hh in range(2):
                qh = jnp.where(is_a if hh == 0 else jnp.logical_not(is_a), qb, 0.0).astype(BF16)
                s = lax.dot_general(qh, kk, _NT, preferred_element_type=F32)
                s = s + bias_ref[br, hh] + pen_tile
                m = jnp.max(s, axis=-1, keepdims=True)
                p = jnp.exp(s - m)
                l = jnp.sum(p, axis=-1, keepdims=True)
                acc = jnp.dot(p.astype(BF16), vv, preferred_element_type=F32)
                outs.append((acc, m, l))
            (acc_a, m_a, l_a), (acc_b, m_b, l_b) = outs
            acc_scr[br, rows, :] = jnp.where(is_a, acc_a, acc_b)
            m_scr[br, rows, :] = jnp.where(is_a, m_a, m_b)
            l_scr[br, rows, :] = jnp.where(is_a, l_a, l_b)
            return carry

        lax.fori_loop(0, dil * nb, block_step, 0, unroll=_ATTN_UNROLL)

    chunk = 256
    for c0 in range(0, seq, chunk):
        sl = pl.ds(c0, chunk)
        m1, m2, m3 = m_scr[0, sl, :], m_scr[1, sl, :], m_scr[2, sl, :]
        mm = jnp.maximum(jnp.maximum(m1, m2), m3)
        w1, w2, w3 = jnp.exp(m1 - mm), jnp.exp(m2 - mm), jnp.exp(m3 - mm)
        num = w1 * acc_scr[0, sl, :] + w2 * acc_scr[1, sl, :] + w3 * acc_scr[2, sl, :]
        den = w1 * l_scr[0, sl, :] + w2 * l_scr[1, sl, :] + w3 * l_scr[2, sl, :]
        o_ref[sl, :] = num / den


def _attn_prompt(q, k, v):
    B, S, _ = q.shape
    assert S == BRANCHES[-1][1] * BLK, "dilated blocks assume SEQ == 16 * 128"
    bias = _prompt_bias_table()
    spec = pl.BlockSpec((None, S, LANES), lambda b, hp: (b, 0, hp))
    scr = pltpu.VMEM((len(BRANCHES), S, LANES), F32)
    return pl.pallas_call(
        functools.partial(_attn_prompt_kernel, seq=S),
        grid=(B, ATT_WIDTH // LANES),
        in_specs=[spec, spec, spec,
                  pl.BlockSpec((None, len(BRANCHES), 2, BLK, 2 * BLK), lambda b, hp: (hp, 0, 0, 0, 0))],
        out_specs=spec,
        out_shape=jax.ShapeDtypeStruct((B, S, ATT_WIDTH), F32),
        scratch_shapes=[scr, scr, scr],
        compiler_params=_cparams("parallel", "parallel"),
        name="attn_prompt",
    )(q, k, v, bias)


def _sample_bias_tables(w_buf, t_new):
    slopes = np.asarray(_alibi_slopes(), np.float64)[:, None, None]
    t = np.arange(t_new)[None, :, None]

    def tables(key_pos):
        delta = (w_buf + t) - key_pos[None, None, :]
        out = []
        for window, dil in BRANCHES:
            ok = (delta >= 0) & (delta <= window) & (delta % dil == 0)
            out.append(np.where(ok, -slopes * delta, NEG_BIG))
        return jnp.asarray(np.stack(out).astype(np.float32))

    return tables(np.arange(w_buf)), tables(w_buf + np.arange(t_new))


def _attn_sample_kernel(q_ref, kn_ref, vn_ref, kt_ref, vt_ref, bias_ref, biasn_ref, o_ref):
    outs = []
    for h in range(N_HEADS):
        hs = slice(h * HEAD_DIM, (h + 1) * HEAD_DIM)
        qh = q_ref[:, hs].astype(BF16)
        knh = kn_ref[:, hs].astype(BF16)
        vnh = vn_ref[:, hs].astype(BF16)
        s_c = jnp.dot(qh, kt_ref[h].astype(BF16), preferred_element_type=F32)
        s_n = lax.dot_general(qh, knh, _NT, preferred_element_type=F32)
        sb_c = [s_c + bias_ref[r, h] for r in range(len(BRANCHES))]
        sb_n = [s_n + biasn_ref[r, h] for r in range(len(BRANCHES))]
        mm = None
        for a in sb_c + sb_n:
            am = jnp.max(a, axis=-1, keepdims=True)
            mm = am if mm is None else jnp.maximum(mm, am)
        p_c = functools.reduce(lambda x, y: x + y, [jnp.exp(a - mm) for a in sb_c])
        p_n = functools.reduce(lambda x, y: x + y, [jnp.exp(a - mm) for a in sb_n])
        den = jnp.sum(p_c, axis=-1, keepdims=True) + jnp.sum(p_n, axis=-1, keepdims=True)
        acc = (lax.dot_general(p_c.astype(BF16), vt_ref[h].astype(BF16), _NT, preferred_element_type=F32)
               + jnp.dot(p_n.astype(BF16), vnh, preferred_element_type=F32))
        outs.append(acc / den)
    o_ref[...] = jnp.concatenate(outs, axis=1)


def _attn_sample(q, kn, vn, cache_k, cache_v):
    B, T, _ = q.shape
    W = cache_k.shape[1]
    assert W >= BRANCHES[-1][0], "every dilated key of a sample query must lie inside the window buffer"
    bias, bias_new = _sample_bias_tables(W, T)
    new = pl.BlockSpec((None, T, ATT_WIDTH), lambda b: (b, 0, 0))
    buf = pl.BlockSpec((None, N_HEADS, HEAD_DIM, W), lambda b: (b, 0, 0, 0))
    const = lambda a: pl.BlockSpec(a.shape, lambda b: (0, 0, 0, 0))
    return pl.pallas_call(
        _attn_sample_kernel,
        grid=(B,),
        in_specs=[new, new, new, buf, buf, const(bias), const(bias_new)],
        out_specs=new,
        out_shape=jax.ShapeDtypeStruct((B, T, ATT_WIDTH), F32),
        compiler_params=_cparams("parallel"),
        name="attn_sample",
    )(q, kn, vn, cache_k.transpose(0, 2, 3, 1), cache_v.transpose(0, 2, 3, 1), bias, bias_new)


def _ssm_prep_kernel(lr_ref, li_ref, ldt_ref, btr_ref, bti_ref, ctr_ref, cti_ref, a_ref, bz_ref, cb_ref):
    lr, li = lr_ref[...], li_ref[...]
    dt = jnp.exp(ldt_ref[...])
    mag = jnp.exp(lr * dt)
    a_re = mag * jnp.cos(li * dt)
    a_im = mag * jnp.sin(li * dt)
    den = lr * lr + li * li
    nr = a_re - 1.0
    z_re = (nr * lr + a_im * li) / den
    z_im = (a_im * lr - nr * li) / den
    a_ref[0] = jnp.broadcast_to(a_re, (SUBLANES, SSM_LANES))
    a_ref[1] = jnp.broadcast_to(a_im, (SUBLANES, SSM_LANES))
    btr, bti = btr_ref[...], bti_ref[...]
    bz_re = z_re * btr - z_im * bti
    bz_im = z_re * bti + z_im * btr
    row_g = lax.broadcasted_iota(jnp.int32, (SSM_WIDTH, SSM_LANES), 0) // SSM_GROUP
    col_g = lax.broadcasted_iota(jnp.int32, (SSM_WIDTH, SSM_LANES), 1) // SSM_STATE
    diag = row_g == col_g

    def block_diag(small):
        tiled = jnp.concatenate([small] * N_SSM_GROUPS, axis=0)
        return jnp.where(diag, tiled, 0.0).astype(BF16)

    bz_ref[0] = block_diag(bz_re)
    bz_ref[1] = block_diag(bz_im)
    cb_ref[0] = block_diag(ctr_ref[...])
    cb_ref[1] = block_diag(cti_ref[...])


def _ssm_prep(lam_re, lam_im, log_dt, b_re, b_im, c_re, c_im):
    row = lambda t: t.reshape(1, SSM_LANES)
    chan = lambda t: t.reshape(SSM_GROUP, SSM_LANES)
    args = (row(lam_re), row(lam_im), row(jnp.repeat(log_dt, SSM_STATE)),
            chan(b_re.transpose(2, 0, 1)), chan(b_im.transpose(2, 0, 1)),
            chan(c_re.transpose(1, 0, 2)), chan(c_im.transpose(1, 0, 2)))
    return pl.pallas_call(
        _ssm_prep_kernel,
        out_shape=[jax.ShapeDtypeStruct((2, SUBLANES, SSM_LANES), F32),
                   jax.ShapeDtypeStruct((2, SSM_WIDTH, SSM_LANES), BF16),
                   jax.ShapeDtypeStruct((2, SSM_WIDTH, SSM_LANES), BF16)],
        compiler_params=pltpu.CompilerParams(vmem_limit_bytes=VMEM_LIMIT),
        name="ssm_prep",
    )(*args)


_SSM_LANE_CHUNK = 512


def _ssm_prompt_kernel(u_ref, a_ref, bz_ref, cb_ref, d_ref, z_ref, hre_ref, him_ref, xr, xi, hst, *, tc):
    @pl.when(pl.program_id(1) == 0)
    def _():
        hst[...] = jnp.zeros_like(hst)

    u = u_ref[...]
    ub = u.astype(BF16)
    xr[...] = jnp.dot(ub, bz_ref[0], preferred_element_type=F32)
    xi[...] = jnp.dot(ub, bz_ref[1], preferred_element_type=F32)
    for lc in range(SSM_LANES // _SSM_LANE_CHUNK):
        sl = slice(lc * _SSM_LANE_CHUNK, (lc + 1) * _SSM_LANE_CHUNK)
        ar, ai = a_ref[0, :, sl], a_ref[1, :, sl]

        def step(t, carry, sl=sl, ar=ar, ai=ai):
            hr, hi = carry
            rows = pl.ds(pl.multiple_of(t * SUBLANES, SUBLANES), SUBLANES)
            nhr = ar * hr - ai * hi + xr[rows, sl]
            nhi = ar * hi + ai * hr + xi[rows, sl]
            xr[rows, sl] = nhr
            xi[rows, sl] = nhi
            return nhr, nhi

        hr, hi = lax.fori_loop(0, tc, step, (hst[0, :, sl], hst[1, :, sl]))
        hst[0, :, sl] = hr
        hst[1, :, sl] = hi
    y = (lax.dot_general(xr[...].astype(BF16), cb_ref[0], _NT, preferred_element_type=F32)
         - lax.dot_general(xi[...].astype(BF16), cb_ref[1], _NT, preferred_element_type=F32)
         + d_ref[...] * u)
    z_ref[...] = _gelu(y)
    hre_ref[...] = hst[0]
    him_ref[...] = hst[1]


def _ssm_prompt(u_tm, a, bz, cb, d_row, tc=64):
    G, rows_total, _ = u_tm.shape
    S = rows_total // SUBLANES
    rows = tc * SUBLANES
    full3 = lambda shp: pl.BlockSpec(shp, lambda g, t: (0, 0, 0))
    st_spec = pl.BlockSpec((SUBLANES, SSM_LANES), lambda g, t: (g, 0))
    st_sds = jax.ShapeDtypeStruct((G * SUBLANES, SSM_LANES), F32)
    return pl.pallas_call(
        functools.partial(_ssm_prompt_kernel, tc=tc),
        grid=(G, S // tc),
        in_specs=[pl.BlockSpec((None, rows, SSM_WIDTH), lambda g, t: (g, t, 0)),
                  full3((2, SUBLANES, SSM_LANES)), full3((2, SSM_WIDTH, SSM_LANES)),
                  full3((2, SSM_WIDTH, SSM_LANES)),
                  pl.BlockSpec((1, SSM_WIDTH), lambda g, t: (0, 0))],
        out_specs=[pl.BlockSpec((None, rows, SSM_WIDTH), lambda g, t: (g, t, 0)), st_spec, st_spec],
        out_shape=[jax.ShapeDtypeStruct(u_tm.shape, F32), st_sds, st_sds],
        scratch_shapes=[pltpu.VMEM((rows, SSM_LANES), F32), pltpu.VMEM((rows, SSM_LANES), F32),
                        pltpu.VMEM((2, SUBLANES, SSM_LANES), F32)],
        compiler_params=_cparams("parallel", "arbitrary"),
        name="ssm_prompt",
    )(u_tm, a, bz, cb, d_row)


def _ssm_sample_kernel(u_ref, h0r_ref, h0i_ref, a_ref, bz_ref, cb_ref, d_ref, z_ref, hre_ref, him_ref,
                       xr, xi, *, t_new):
    u = u_ref[...]
    ub = u.astype(BF16)
    nb = h0r_ref.shape[0]
    n_tiles = SSM_LANES // LANES
    x_re = jnp.dot(ub, bz_ref[0], preferred_element_type=F32)
    x_im = jnp.dot(ub, bz_ref[1], preferred_element_type=F32)
    for j in range(n_tiles):
        xr[j] = x_re[:, j * LANES:(j + 1) * LANES]
        xi[j] = x_im[:, j * LANES:(j + 1) * LANES]
    for j in range(n_tiles):
        cs = slice(j * LANES, (j + 1) * LANES)
        ar = jnp.broadcast_to(a_ref[0, 0:1, cs], (nb, LANES))
        ai = jnp.broadcast_to(a_ref[1, 0:1, cs], (nb, LANES))
        hr, hi = h0r_ref[:, cs], h0i_ref[:, cs]
        for t in range(t_new):
            rows = pl.ds(t, nb, stride=t_new)
            nhr = ar * hr - ai * hi + xr[j, rows, :]
            nhi = ar * hi + ai * hr + xi[j, rows, :]
            xr[j, rows, :] = nhr
            xi[j, rows, :] = nhi
            hr, hi = nhr, nhi
        hre_ref[:, cs] = hr
        him_ref[:, cs] = hi
    h_re = jnp.concatenate([xr[j] for j in range(n_tiles)], axis=1).astype(BF16)
    h_im = jnp.concatenate([xi[j] for j in range(n_tiles)], axis=1).astype(BF16)
    y = (lax.dot_general(h_re, cb_ref[0], _NT, preferred_element_type=F32)
         - lax.dot_general(h_im, cb_ref[1], _NT, preferred_element_type=F32)
         + d_ref[...] * u)
    z_ref[...] = _gelu(y)


def _ssm_sample(u, h0r, h0i, a, bz, cb, d_row, t_new):
    n_tok = u.shape[0]
    nb = h0r.shape[0]
    slab = pltpu.VMEM((SSM_LANES // LANES, n_tok, LANES), F32)
    st = jax.ShapeDtypeStruct((nb, SSM_LANES), F32)
    return pl.pallas_call(
        functools.partial(_ssm_sample_kernel, t_new=t_new),
        out_shape=[jax.ShapeDtypeStruct((n_tok, SSM_WIDTH), F32), st, st],
        scratch_shapes=[slab, slab],
        compiler_params=pltpu.CompilerParams(vmem_limit_bytes=VMEM_LIMIT),
        name="ssm_sample",
    )(u, h0r, h0i, a, bz, cb, d_row)


def _tail_kernel(x_ref, att_ref, z_ref, p_ref, wglu_ref, bglu_ref, gatt_ref, gssm_ref, wout_ref, bout_ref,
                 g1_ref, b1_ref, wple_ref, wgate_ref, h_ref, ht_ref, ple_ref):
    z = z_ref[...]
    gate = jnp.dot(z.astype(BF16), wglu_ref[...], preferred_element_type=F32) + bglu_ref[...]
    ssm_out = z * _sigmoid(gate)
    att = att_ref[...]
    rms_a = att * lax.rsqrt(jnp.mean(att * att, axis=-1, keepdims=True) + LN_EPS) * gatt_ref[...]
    rms_s = ssm_out * lax.rsqrt(jnp.mean(ssm_out * ssm_out, axis=-1, keepdims=True) + LN_EPS) * gssm_ref[...]
    mix = (jnp.dot(rms_a.astype(BF16), wout_ref[:ATT_WIDTH, :], preferred_element_type=F32)
           + jnp.dot(rms_s.astype(BF16), wout_ref[ATT_WIDTH:, :], preferred_element_type=F32)
           + bout_ref[...])
    pre = ALPHA * x_ref[...] + mix
    mu = jnp.mean(pre, axis=-1, keepdims=True)
    cen = pre - mu
    var = jnp.mean(cen * cen, axis=-1, keepdims=True)
    h = cen * lax.rsqrt(var + LN_EPS) * g1_ref[...] + b1_ref[...]
    hb = h.astype(BF16)
    h_ref[...] = h
    ht_ref[...] = h.T.astype(BF16)
    gate2 = _sigmoid(jnp.dot(hb, wgate_ref[...], preferred_element_type=F32))
    ple_ref[...] = gate2 * jnp.dot(p_ref[...].astype(BF16), wple_ref[...], preferred_element_type=F32)


def _tail(x3, att3, z_arr, p3, weights, z_time_major):
    B, S, D = x3.shape
    tm = _MXU_COLS
    tok = lambda w: pl.BlockSpec((None, tm, w), lambda b, t: (b, t, 0))
    if z_time_major:
        z_arr = z_arr.reshape(B // SUBLANES, S, SUBLANES * SSM_WIDTH)
        z_spec = pl.BlockSpec((None, tm, SSM_WIDTH), lambda b, t: (b // SUBLANES, t, b % SUBLANES))
    else:
        z_spec = tok(SSM_WIDTH)
    full = lambda a: pl.BlockSpec(a.shape, lambda b, t: (0, 0))
    return pl.pallas_call(
        _tail_kernel,
        grid=(B, S // tm),
        in_specs=[tok(D), tok(ATT_WIDTH), z_spec, tok(PLE_DIM)] + [full(w) for w in weights],
        out_specs=[tok(D), pl.BlockSpec((None, D, tm), lambda b, t: (b * (S // tm) + t, 0, 0)), tok(D)],
        out_shape=[jax.ShapeDtypeStruct((B, S, D), F32), jax.ShapeDtypeStruct((B * S // tm, D, tm), BF16),
                   jax.ShapeDtypeStruct((B, S, D), F32)],
        compiler_params=_cparams("parallel", "parallel"),
        name="tail",
    )(x3, att3, z_arr, p3, *weights)


def _split_bf16(x):
    hi = x.astype(BF16)
    lo = (x - hi.astype(F32)).astype(BF16)
    return hi, lo


def _peer_prep_kernel(wpq_ref, sk1_ref, sk2_ref, m1_ref, m2_ref):
    def combined(sk, w):
        sk_hi, sk_lo = _split_bf16(sk)
        w_hi, w_lo = _split_bf16(w)
        dot = lambda a, b: lax.dot_general(a, b, _NT, preferred_element_type=F32)
        return dot(sk_hi, w_hi) + (dot(sk_hi, w_lo) + dot(sk_lo, w_hi))

    half = D_KEY // 2
    m1_ref[...] = combined(sk1_ref[...], wpq_ref[:, :half]).astype(BF16)
    m2_ref[...] = combined(sk2_ref[...], wpq_ref[:, half:]).astype(BF16)


def _peer_prep(w_pq, sk1, sk2):
    half = D_KEY // 2
    m1, m2 = pl.pallas_call(
        _peer_prep_kernel,
        grid=(PEER_HEADS,),
        in_specs=[pl.BlockSpec((D_MODEL, D_KEY), lambda h: (0, h)),
                  pl.BlockSpec((N_KEYS, half), lambda h: (0, 0)),
                  pl.BlockSpec((N_KEYS, half), lambda h: (0, 0))],
        out_specs=[pl.BlockSpec((N_KEYS, D_MODEL), lambda h: (0, h)),
                   pl.BlockSpec((None, N_KEYS, D_MODEL), lambda h: (h, 0, 0))],
        out_shape=[jax.ShapeDtypeStruct((N_KEYS, PEER_HEADS * D_MODEL), BF16),
                   jax.ShapeDtypeStruct((PEER_HEADS, N_KEYS, D_MODEL), BF16)],
        compiler_params=_cparams("parallel"),
        name="peer_prep",
    )(w_pq, sk1, sk2)
    return jnp.concatenate([m1.reshape(N_KEYS * PEER_HEADS, D_MODEL),
                            m2.reshape(PEER_HEADS * N_KEYS, D_MODEL)], axis=0)


def _vmax(a, b):
    return b if a is None else a if b is None else jnp.maximum(a, b)


def _vmin(a, b):
    return None if a is None or b is None else jnp.minimum(a, b)


def _larger_first(xs, i, l):
    a, b = xs[i], xs[l]
    xs[i], xs[l] = _vmax(a, b), _vmin(a, b)


def _sort_desc(xs):
    xs = list(xs)
    n, k = len(xs), 2
    while k <= n:
        j = k // 2
        while j >= 1:
            for i in range(n):
                l = i ^ j
                if l > i:
                    if (i & k) == 0:
                        _larger_first(xs, i, l)
                    else:
                        _larger_first(xs, l, i)
            j //= 2
        k *= 2
    return xs


def _merge_top(a, b):
    n = len(a)
    c = [_vmax(a[i], b[n - 1 - i]) for i in range(n)]
    j = n // 2
    while j >= 1:
        for i in range(n):
            l = i ^ j
            if l > i:
                _larger_first(c, i, l)
        j //= 2
    return c


def _top_of_lists(lists):
    lists = [l + [None] * (PEER_TOPK - len(l)) for l in lists]
    while len(lists) > 1:
        nxt = [_merge_top(lists[i], lists[i + 1]) for i in range(0, len(lists) - 1, 2)]
        if len(lists) % 2:
            nxt.append(lists[-1])
        lists = nxt
    return lists[0]


_CAND_PAIRS = [(a, b) for a in range(PEER_TOPK) for b in range(PEER_TOPK) if (a + 1) * (b + 1) <= PEER_TOPK]
_CAND_SOLO_ROWS = 5


def _peer_gate_kernel(ht_ref, mt_ref, e1_ref, e2_ref, tau_ref, *, tn):
    st = jnp.dot(mt_ref[...], ht_ref[...], preferred_element_type=F32)
    n1 = N_KEYS * PEER_HEADS
    rowid = lax.broadcasted_iota(jnp.int32, (SUBLANES, LANES), 0)
    for jt in range(tn // LANES):
        cs = slice(jt * LANES, (jt + 1) * LANES)
        s1 = [st[i * PEER_HEADS:(i + 1) * PEER_HEADS, cs] for i in range(N_KEYS)]
        top1 = _top_of_lists([_sort_desc(s1[g:g + PEER_TOPK]) for g in range(0, N_KEYS, PEER_TOPK)])
        max1 = top1[0]
        for i in range(N_KEYS):
            e1_ref[jt, i * PEER_HEADS:(i + 1) * PEER_HEADS, :] = jnp.exp(s1[i] - max1)
        v1 = [jnp.exp(t - max1) for t in top1]
        e2_heads, v2_heads = [], []
        for h in range(PEER_HEADS):
            s2 = st[n1 + h * N_KEYS:n1 + (h + 1) * N_KEYS, cs]
            top = _sort_desc([s2[r * SUBLANES:(r + 1) * SUBLANES] for r in range(N_KEYS // SUBLANES)])
            for shift in (4, 2, 1):
                top = _merge_top(top, [pltpu.roll(t, shift, 0) for t in top])
            max2 = top[0]
            e2_heads.append(jnp.exp(s2 - max2[0:1, :]))
            v2_heads.append([jnp.exp(t - max2) for t in top])
        v2 = []
        for b in range(PEER_TOPK):
            acc = v2_heads[0][b]
            for h in range(1, PEER_HEADS):
                acc = jnp.where(rowid == h, v2_heads[h][b], acc)
            v2.append(acc)
        cand0 = {(a, b): v1[a] * v2[b] for a, b in _CAND_PAIRS}
        lists = [[cand0[(a, b)] for b in range(PEER_TOPK) if (a, b) in cand0] for a in range(_CAND_SOLO_ROWS)]
        lists += [[cand0[(a, b)] for a in range(_CAND_SOLO_ROWS, PEER_TOPK) if (a, b) in cand0]
                  for b in range(PEER_TOPK // (_CAND_SOLO_ROWS + 1))]
        assert sum(len(l) for l in lists) == len(_CAND_PAIRS)
        top = _top_of_lists(lists)
        tau0 = top[PEER_TOPK - 1]
        rz = 1.0 / functools.reduce(lambda x, y: x + y, top)
        v2n = [v * rz for v in v2]
        tau = None
        for (a, b), c0 in cand0.items():
            c = jnp.where(c0 >= tau0, v1[a] * v2n[b], jnp.inf)
            tau = c if tau is None else jnp.minimum(tau, c)
        tau_ref[jt] = tau
        for h in range(PEER_HEADS):
            e2_ref[jt, h * N_KEYS:(h + 1) * N_KEYS, :] = e2_heads[h] * rz[h:h + 1, :]


def _peer_gate(ht, mt):
    tn = _MXU_COLS
    n_tok = ht.shape[0] * tn
    nt = tn // LANES
    slab = lambda rows: pl.BlockSpec((nt, rows, LANES), lambda t: (t, 0, 0))
    return pl.pallas_call(
        functools.partial(_peer_gate_kernel, tn=tn),
        grid=(n_tok // tn,),
        in_specs=[pl.BlockSpec((None, D_MODEL, tn), lambda t: (t, 0, 0)),
                  pl.BlockSpec(mt.shape, lambda t: (0, 0))],
        out_specs=[slab(N_KEYS * PEER_HEADS), slab(N_KEYS * PEER_HEADS), slab(SUBLANES)],
        out_shape=[jax.ShapeDtypeStruct((n_tok // LANES, N_KEYS * PEER_HEADS, LANES), F32),
                   jax.ShapeDtypeStruct((n_tok // LANES, N_KEYS * PEER_HEADS, LANES), F32),
                   jax.ShapeDtypeStruct((n_tok // LANES, SUBLANES, LANES), F32)],
        compiler_params=_cparams("parallel"),
        name="peer_gate",
    )(ht, mt)


def _gelu_times(x, g):
    c = math.sqrt(2.0 / math.pi)
    hx = 0.5 * x
    t = jnp.tanh(x * (x * x * (c * 0.044715) + c))
    return g * (hx + hx * t)


_KEYS_PER_SLICE = 1


def _peer_dense_kernel(ht_ref, u_ref, vt_ref, e1_ref, e2_ref, tau_ref, o_ref, acc_t, w_even, w_odd, *,
                       tb, ec, n_chunks):
    s = pl.program_id(1)
    n_pairs = tb // _MXU_COLS
    keys_per_chunk = ec // N_KEYS
    slices_per_pair = keys_per_chunk // _KEYS_PER_SLICE
    n_iter = n_pairs * slices_per_pair
    e_rows = _KEYS_PER_SLICE * N_KEYS
    v_rows = D_MODEL // slices_per_pair
    chunk = jnp.minimum(s, n_chunks - 1)

    @pl.when(s == 0)
    def _():
        acc_t[...] = jnp.zeros_like(acc_t)
        w_odd[...] = jnp.zeros_like(w_odd)

    def scores(it):
        pr, sl = it // slices_per_pair, it % slices_per_pair
        return jnp.dot(u_ref[sl * e_rows:(sl + 1) * e_rows, :], ht_ref[pr], preferred_element_type=F32)

    def step(w_in, w_out):
        a_next = scores(0)
        for it in range(n_iter):
            pr, sl = it // slices_per_pair, it % slices_per_pair
            a = a_next
            if it + 1 < n_iter:
                a_next = scores(it + 1)
            vrows = slice(sl * v_rows, (sl + 1) * v_rows)
            acc_t[pr, vrows, :] += jnp.dot(vt_ref[vrows, :], w_in[pr], preferred_element_type=F32)
            halves = []
            for half in range(_MXU_COLS // LANES):
                j = pr * (_MXU_COLS // LANES) + half
                cs = slice(half * LANES, (half + 1) * LANES)
                taub = [jnp.broadcast_to(tau_ref[j, pl.ds(h, 1), :], (SUBLANES, LANES)) for h in range(PEER_HEADS)]
                pieces = []
                for kk in range(_KEYS_PER_SLICE):
                    k1 = chunk * keys_per_chunk + sl * _KEYS_PER_SLICE + kk
                    e1b = [jnp.broadcast_to(e1_ref[j, pl.ds(k1 * PEER_HEADS + h, 1), :], (SUBLANES, LANES))
                           for h in range(PEER_HEADS)]
                    for kb in range(N_KEYS // SUBLANES):
                        gate = None
                        for h in range(PEER_HEADS):
                            prod = e1b[h] * e2_ref[j, pl.ds(h * N_KEYS + kb * SUBLANES, SUBLANES), :]
                            sel = jnp.where(prod >= taub[h], prod, 0.0)
                            gate = sel if gate is None else gate + sel
                        r0 = kk * N_KEYS + kb * SUBLANES
                        pieces.append(_gelu_times(a[r0:r0 + SUBLANES, cs], gate))
                halves.append(jnp.concatenate(pieces, axis=0))
            w_out[pr, sl * e_rows:(sl + 1) * e_rows, :] = jnp.concatenate(halves, axis=1).astype(BF16)

    @pl.when(s % 2 == 0)
    def _():
        step(w_odd, w_even)

    @pl.when(s % 2 == 1)
    def _():
        step(w_even, w_odd)

    @pl.when(s == n_chunks)
    def _():
        for pr in range(n_pairs):
            o_ref[pr * _MXU_COLS:(pr + 1) * _MXU_COLS, :] = acc_t[pr].T


def _peer_dense(ht, u_bf, vt_bf, e1, e2, tau, tb=1024, ec=512):
    n_tok = ht.shape[0] * _MXU_COLS
    tb = min(tb, n_tok)
    nt = tb // LANES
    n_chunks = N_EXPERTS // ec
    slab = lambda rows: pl.BlockSpec((nt, rows, LANES), lambda t, s: (t, 0, 0))
    w_buf = pltpu.VMEM((tb // _MXU_COLS, ec, _MXU_COLS), BF16)
    return pl.pallas_call(
        functools.partial(_peer_dense_kernel, tb=tb, ec=ec, n_chunks=n_chunks),
        grid=(n_tok // tb, n_chunks + 1),
        in_specs=[pl.BlockSpec((tb // _MXU_COLS, D_MODEL, _MXU_COLS), lambda t, s: (t, 0, 0)),
                  pl.BlockSpec((ec, D_MODEL), lambda t, s: (jnp.minimum(s, n_chunks - 1), 0)),
                  pl.BlockSpec((D_MODEL, ec), lambda t, s: (0, jnp.maximum(s - 1, 0))),
                  slab(N_KEYS * PEER_HEADS), slab(N_KEYS * PEER_HEADS), slab(SUBLANES)],
        out_specs=pl.BlockSpec((tb, D_MODEL), lambda t, s: (t, 0)),
        out_shape=jax.ShapeDtypeStruct((n_tok, D_MODEL), F32),
        scratch_shapes=[pltpu.VMEM((tb // _MXU_COLS, D_MODEL, _MXU_COLS), F32), w_buf, w_buf],
        compiler_params=_cparams("parallel", "arbitrary"),
        name="peer_dense",
    )(ht, u_bf, vt_bf, e1, e2, tau)


def _final_kernel(h_ref, f_ref, p_ref, g_ref, b_ref, o_ref):
    pre = ALPHA * h_ref[...] + f_ref[...] + p_ref[...]
    mu = jnp.mean(pre, axis=-1, keepdims=True)
    cen = pre - mu
    var = jnp.mean(cen * cen, axis=-1, keepdims=True)
    o_ref[...] = cen * lax.rsqrt(var + LN_EPS) * g_ref[...] + b_ref[...]


def _final(h2, ffn2, ple2, g, b, tm=512):
    n_tok, D = h2.shape
    tm = min(tm, n_tok)
    tok = pl.BlockSpec((tm, D), lambda t: (t, 0))
    vec = pl.BlockSpec((1, D), lambda t: (0, 0))
    return pl.pallas_call(
        _final_kernel,
        grid=(n_tok // tm,),
        in_specs=[tok, tok, tok, vec, vec],
        out_specs=tok,
        out_shape=jax.ShapeDtypeStruct((n_tok, D), F32),
        compiler_params=_cparams("parallel"),
        name="final_norm",
    )(h2, ffn2, ple2, g, b)


def _ffn_and_norm(h3, ht, ple3, mt, u_bf, vt_bf, ln2_g, ln2_b):
    B, S, D = h3.shape
    n_tok = B * S
    e1, e2, tau = _peer_gate(ht, mt)
    ffn = _peer_dense(ht, u_bf, vt_bf, e1, e2, tau)
    y = _final(h3.reshape(n_tok, D), ffn, ple3.reshape(n_tok, D), ln2_g, ln2_b)
    return y.reshape(B, S, D)


def kernel(x_prompt, x_sample, cache_k, cache_v, state_ssm_re, state_ssm_im, p_prompt, p_sample,
           w_in, b_in, lam_re, lam_im, log_dt, ssm_b_re, ssm_b_im, ssm_c_re, ssm_c_im, ssm_d,
           w_glu, b_glu, g_att_out, g_ssm_out, w_out, b_out, ln1_g, ln1_b,
           w_pq, sub_keys1, sub_keys2, peer_u, peer_v, w_ple, w_ple_gate, ln2_g, ln2_b):
    assert w_in.shape[0] == DEPTH == 1
    B, S, D = x_prompt.shape
    Bs, Ts, _ = x_sample.shape
    row = lambda t: t.reshape(1, -1)

    w_in_bf = w_in[0].astype(BF16)
    b_in2 = row(b_in[0])
    a, bz, cb = _ssm_prep(lam_re[0], lam_im[0], log_dt[0], ssm_b_re[0], ssm_b_im[0], ssm_c_re[0], ssm_c_im[0])
    d_row = row(ssm_d[0])
    tail_w = (w_glu[0].astype(BF16), row(b_glu[0]), row(g_att_out[0]), row(g_ssm_out[0]),
              w_out[0].astype(BF16), row(b_out[0]), row(ln1_g[0]), row(ln1_b[0]),
              w_ple[0].astype(BF16), w_ple_gate[0].astype(BF16))
    mt = _peer_prep(w_pq[0], sub_keys1[0], sub_keys2[0])
    u_bf = peer_u[0].astype(BF16)
    vt_bf = peer_v[0].astype(BF16).T
    g2, b2 = row(ln2_g[0]), row(ln2_b[0])

    q, k_p, v_p, u_tm, kt_p, vt_p = _in_proj(x_prompt, w_in_bf, b_in2, prompt=True)
    att = _attn_prompt(q, k_p, v_p)
    z_tm, hre_p, him_p = _ssm_prompt(u_tm.reshape(B // SUBLANES, S * SUBLANES, SSM_WIDTH), a, bz, cb, d_row)
    h3, hb3, ple3 = _tail(x_prompt, att, z_tm, p_prompt[0], tail_w, z_time_major=True)
    y_p = _ffn_and_norm(h3, hb3, ple3, mt, u_bf, vt_bf, g2, b2)

    n_s = Bs * Ts
    q_s, k_s, v_s, u_s = _in_proj(x_sample.reshape(1, n_s, D), w_in_bf, b_in2, prompt=False)
    per_seq = lambda t: t.reshape(Bs, Ts, ATT_WIDTH)
    att_s = _attn_sample(per_seq(q_s), per_seq(k_s), per_seq(v_s), cache_k[0], cache_v[0])
    z_s, hre_s, him_s = _ssm_sample(u_s.reshape(n_s, SSM_WIDTH), state_ssm_re[0].reshape(Bs, SSM_LANES),
                                    state_ssm_im[0].reshape(Bs, SSM_LANES), a, bz, cb, d_row, Ts)
    h3s, hb3s, ple3s = _tail(x_sample.reshape(1, n_s, D), att_s.reshape(1, n_s, ATT_WIDTH),
                             z_s.reshape(1, n_s, SSM_WIDTH), p_sample[0].reshape(1, n_s, PLE_DIM), tail_w,
                             z_time_major=False)
    y_s = _ffn_and_norm(h3s, hb3s, ple3s, mt, u_bf, vt_bf, g2, b2).reshape(Bs, Ts, D)

    w_keep = min(BRANCHES[-1][0], S)
    heads = lambda t, b, s: t.reshape(1, b, s, N_HEADS, HEAD_DIM)
    window = lambda t: (t.reshape(B, N_HEADS, HEAD_DIM, S)[..., S - w_keep:].transpose(0, 3, 1, 2)
                        .reshape(1, B, w_keep, N_HEADS, HEAD_DIM))
    state = lambda t, b: t.reshape(1, b, N_SSM_GROUPS, SSM_STATE)
    return (y_p, y_s,
            window(kt_p), window(vt_p),
            state(hre_p, B), state(him_p, B),
            heads(k_s, Bs, Ts), heads(v_s, Bs, Ts),
            state(hre_s, Bs), state(him_s, Bs))
```

```python
import functools
import math

import jax
import jax.numpy as jnp
import numpy as np
from jax import lax
from jax.experimental import pallas as pl
from jax.experimental.pallas import tpu as pltpu

D_MODEL = 1024
ATT_WIDTH = 512
SSM_WIDTH = 512
HEAD_DIM = 64
N_HEADS = 8
BRANCHES = ((128, 1), (512, 4), (2048, 16))
BLK = 128
SSM_GROUP = 16
N_SSM_GROUPS = 32
SSM_STATE = 64
SSM_LANES = N_SSM_GROUPS * SSM_STATE
PEER_HEADS = 8
N_KEYS = 128
N_EXPERTS = N_KEYS * N_KEYS
PEER_TOPK = 16
D_KEY = 256
PLE_DIM = 256
DEPTH = 1
ALPHA = (2 * DEPTH) ** 0.25
LN_EPS = 1e-5

LANES = 128
SUBLANES = 8
_MXU_COLS = 2 * LANES
NEG_BIG = -1e30
VMEM_LIMIT = 56 * 1024 * 1024

F32 = jnp.float32
BF16 = jnp.bfloat16

_NT = (((1,), (1,)), ((), ()))


def _cparams(*sem):
    return pltpu.CompilerParams(dimension_semantics=sem, vmem_limit_bytes=VMEM_LIMIT)


def _gelu(x):
    c = math.sqrt(2.0 / math.pi)
    return 0.5 * x * (1.0 + jnp.tanh(c * (x + 0.044715 * (x * x * x))))


def _sigmoid(x):
    return 1.0 / (1.0 + jnp.exp(-x))


def _in_proj_kernel(x_ref, w_ref, b_ref, q_ref, k_ref, v_ref, u_ref, *kv_t_refs):
    x = x_ref[...].astype(BF16)
    proj = jnp.dot(x, w_ref[...], preferred_element_type=F32) + b_ref[...]
    q_ref[...] = proj[:, :ATT_WIDTH] * (HEAD_DIM ** -0.5)
    k = proj[:, ATT_WIDTH:2 * ATT_WIDTH]
    v = proj[:, 2 * ATT_WIDTH:3 * ATT_WIDTH]
    k_ref[...] = k
    v_ref[...] = v
    u_ref[...] = proj[:, 3 * ATT_WIDTH:]
    if kv_t_refs:
        kv_t_refs[0][...] = k.T
        kv_t_refs[1][...] = v.T


def _in_proj(x3, w_bf, b2, prompt, tm=512):
    B, S, D = x3.shape
    n_out = w_bf.shape[1]
    tm = min(tm, S)
    spec = pl.BlockSpec((None, tm, ATT_WIDTH), lambda b, t: (b, t, 0))
    sds = jax.ShapeDtypeStruct((B, S, ATT_WIDTH), F32)
    out_specs, out_shape = [spec, spec, spec], [sds, sds, sds]
    if prompt:
        out_specs.append(pl.BlockSpec((None, tm, SSM_WIDTH), lambda b, t: (b // SUBLANES, t, b % SUBLANES)))
        out_shape.append(jax.ShapeDtypeStruct((B // SUBLANES, S, SUBLANES * SSM_WIDTH), F32))
        spec_t = pl.BlockSpec((None, ATT_WIDTH, tm), lambda b, t: (b, 0, t))
        out_specs += [spec_t, spec_t]
        out_shape += [jax.ShapeDtypeStruct((B, ATT_WIDTH, S), F32)] * 2
    else:
        out_specs.append(spec)
        out_shape.append(jax.ShapeDtypeStruct((B, S, SSM_WIDTH), F32))
    return pl.pallas_call(
        _in_proj_kernel,
        grid=(B, S // tm),
        in_specs=[pl.BlockSpec((None, tm, D), lambda b, t: (b, t, 0)),
                  pl.BlockSpec((D, n_out), lambda b, t: (0, 0)),
                  pl.BlockSpec((1, n_out), lambda b, t: (0, 0))],
        out_specs=out_specs,
        out_shape=out_shape,
        compiler_params=_cparams("parallel", "parallel"),
        name="in_proj",
    )(x3, w_bf, b2)


def _alibi_slopes():
    return [2.0 ** (-8.0 * (h + 1) / N_HEADS) for h in range(N_HEADS)]


def _prompt_bias_table():
    slopes = jnp.asarray(_alibi_slopes(), F32).reshape(N_HEADS // 2, 1, 2, 1, 1)
    qi = jnp.arange(BLK)[:, None]
    kj = jnp.arange(2 * BLK)[None, :]
    delta = qi - kj + BLK
    tabs = []
    for window, dil in BRANCHES:
        steps = window // dil
        valid = (delta >= 0) & (delta <= steps)
        dist = (dil * delta).astype(F32)
        tabs.append(jnp.where(valid, -slopes * dist, NEG_BIG))
    return jnp.concatenate(tabs, axis=1)


_ATTN_UNROLL = 8


def _attn_prompt_kernel(q_ref, k_ref, v_ref, bias_ref, o_ref, acc_scr, m_scr, l_scr, *, seq):
    lane = lax.broadcasted_iota(jnp.int32, (BLK, LANES), 1)
    is_a = lane < HEAD_DIM

    for br, (_, dil) in enumerate(BRANCHES):
        nb = (seq // dil) // BLK

        def block_step(idx, carry, br=br, dil=dil, nb=nb):
            r = idx // nb
            n = idx % nb
            start = n * (BLK * dil) + r
            pstart = jnp.maximum(n - 1, 0) * (BLK * dil) + r
            if dil == 1:
                rows = pl.ds(pl.multiple_of(start, BLK), BLK)
                prows = pl.ds(pl.multiple_of(pstart, BLK), BLK)
            else:
                rows = pl.ds(start, BLK, stride=dil)
                prows = pl.ds(pstart, BLK, stride=dil)
            qb = q_ref[rows, :]
            kk = jnp.concatenate([k_ref[prows, :], k_ref[rows, :]], axis=0).astype(BF16)
            vv = jnp.concatenate([v_ref[prows, :], v_ref[rows, :]], axis=0).astype(BF16)
            pen = jnp.where(n == 0, NEG_BIG, 0.0).astype(F32)
            pen_tile = jnp.concatenate([jnp.full((BLK, BLK), pen, F32), jnp.zeros((BLK, BLK), F32)], axis=1)
            outs = []
            for hh in range(2):
                qh = jnp.where(is_a if hh == 0 else jnp.logical_not(is_a), qb, 0.0).astype(BF16)
                s = lax.dot_general(qh, kk, _NT, preferred_element_type=F32)
                s = s + bias_ref[br, hh] + pen_tile
                m = jnp.max(s, axis=-1, keepdims=True)
                p = jnp.exp(s - m)
                l = jnp.sum(p, axis=-1, keepdims=True)
                acc = jnp.dot(p.astype(BF16), vv, preferred_element_type=F32)
                outs.append((acc, m, l))
            (acc_a, m_a, l_a), (acc_b, m_b, l_b) = outs
            acc_scr[br, rows, :] = jnp.where(is_a, acc_a, acc_b)
            m_scr[br, rows, :] = jnp.where(is_a, m_a, m_b)
            l_scr[br, rows, :] = jnp.where(is_a, l_a, l_b)
            return carry

        lax.fori_loop(0, dil * nb, block_step, 0, unroll=_ATTN_UNROLL)

    chunk = 256
    for c0 in range(0, seq, chunk):
        sl = pl.ds(c0, chunk)
        m1, m2, m3 = m_scr[0, sl, :], m_scr[1, sl, :], m_scr[2, sl, :]
        mm = jnp.maximum(jnp.maximum(m1, m2), m3)
        w1, w2, w3 = jnp.exp(m1 - mm), jnp.exp(m2 - mm), jnp.exp(m3 - mm)
        num = w1 * acc_scr[0, sl, :] + w2 * acc_scr[1, sl, :] + w3 * acc_scr[2, sl, :]
        den = w1 * l_scr[0, sl, :] + w2 * l_scr[1, sl, :] + w3 * l_scr[2, sl, :]
        o_ref[sl, :] = num / den


def _attn_prompt(q, k, v):
    B, S, _ = q.shape
    assert S == BRANCHES[-1][1] * BLK, "dilated blocks assume SEQ == 16 * 128"
    bias = _prompt_bias_table()
    spec = pl.BlockSpec((None, S, LANES), lambda b, hp: (b, 0, hp))
    scr = pltpu.VMEM((len(BRANCHES), S, LANES), F32)
    return pl.pallas_call(
        functools.partial(_attn_prompt_kernel, seq=S),
        grid=(B, ATT_WIDTH // LANES),
        in_specs=[spec, spec, spec,
                  pl.BlockSpec((None, len(BRANCHES), 2, BLK, 2 * BLK), lambda b, hp: (hp, 0, 0, 0, 0))],
        out_specs=spec,
        out_shape=jax.ShapeDtypeStruct((B, S, ATT_WIDTH), F32),
        scratch_shapes=[scr, scr, scr],
        compiler_params=_cparams("parallel", "parallel"),
        name="attn_prompt",
    )(q, k, v, bias)


def _sample_bias_tables(w_buf, t_new):
    slopes = np.asarray(_alibi_slopes(), np.float64)[:, None, None]
    t = np.arange(t_new)[None, :, None]

    def tables(key_pos):
        delta = (w_buf + t) - key_pos[None, None, :]
        out = []
        for window, dil in BRANCHES:
            ok = (delta >= 0) & (delta <= window) & (delta % dil == 0)
            out.append(np.where(ok, -slopes * delta, NEG_BIG))
        return jnp.asarray(np.stack(out).astype(np.float32))

    return tables(np.arange(w_buf)), tables(w_buf + np.arange(t_new))


def _attn_sample_kernel(q_ref, kn_ref, vn_ref, kt_ref, vt_ref, bias_ref, biasn_ref, o_ref):
    outs = []
    for h in range(N_HEADS):
        hs = slice(h * HEAD_DIM, (h + 1) * HEAD_DIM)
        qh = q_ref[:, hs].astype(BF16)
        knh = kn_ref[:, hs].astype(BF16)
        vnh = vn_ref[:, hs].astype(BF16)
        s_c = jnp.dot(qh, kt_ref[h].astype(BF16), preferred_element_type=F32)
        s_n = lax.dot_general(qh, knh, _NT, preferred_element_type=F32)
        sb_c = [s_c + bias_ref[r, h] for r in range(len(BRANCHES))]
        sb_n = [s_n + biasn_ref[r, h] for r in range(len(BRANCHES))]
        mm = None
        for a in sb_c + sb_n:
            am = jnp.max(a, axis=-1, keepdims=True)
            mm = am if mm is None else jnp.maximum(mm, am)
        p_c = functools.reduce(lambda x, y: x + y, [jnp.exp(a - mm) for a in sb_c])
        p_n = functools.reduce(lambda x, y: x + y, [jnp.exp(a - mm) for a in sb_n])
        den = jnp.sum(p_c, axis=-1, keepdims=True) + jnp.sum(p_n, axis=-1, keepdims=True)
        acc = (lax.dot_general(p_c.astype(BF16), vt_ref[h].astype(BF16), _NT, preferred_element_type=F32)
               + jnp.dot(p_n.astype(BF16), vnh, preferred_element_type=F32))
        outs.append(acc / den)
    o_ref[...] = jnp.concatenate(outs, axis=1)


def _attn_sample(q, kn, vn, cache_k, cache_v):
    B, T, _ = q.shape
    W = cache_k.shape[1]
    assert W >= BRANCHES[-1][0], "every dilated key of a sample query must lie inside the window buffer"
    bias, bias_new = _sample_bias_tables(W, T)
    new = pl.BlockSpec((None, T, ATT_WIDTH), lambda b: (b, 0, 0))
    buf = pl.BlockSpec((None, N_HEADS, HEAD_DIM, W), lambda b: (b, 0, 0, 0))
    const = lambda a: pl.BlockSpec(a.shape, lambda b: (0, 0, 0, 0))
    return pl.pallas_call(
        _attn_sample_kernel,
        grid=(B,),
        in_specs=[new, new, new, buf, buf, const(bias), const(bias_new)],
        out_specs=new,
        out_shape=jax.ShapeDtypeStruct((B, T, ATT_WIDTH), F32),
        compiler_params=_cparams("parallel"),
        name="attn_sample",
    )(q, kn, vn, cache_k.transpose(0, 2, 3, 1), cache_v.transpose(0, 2, 3, 1), bias, bias_new)


def _ssm_prep_kernel(lr_ref, li_ref, ldt_ref, btr_ref, bti_ref, ctr_ref, cti_ref, a_ref, bz_ref, cb_ref):
    lr, li = lr_ref[...], li_ref[...]
    dt = jnp.exp(ldt_ref[...])
    mag = jnp.exp(lr * dt)
    a_re = mag * jnp.cos(li * dt)
    a_im = mag * jnp.sin(li * dt)
    den = lr * lr + li * li
    nr = a_re - 1.0
    z_re = (nr * lr + a_im * li) / den
    z_im = (a_im * lr - nr * li) / den
    a_ref[0] = jnp.broadcast_to(a_re, (SUBLANES, SSM_LANES))
    a_ref[1] = jnp.broadcast_to(a_im, (SUBLANES, SSM_LANES))
    btr, bti = btr_ref[...], bti_ref[...]
    bz_re = z_re * btr - z_im * bti
    bz_im = z_re * bti + z_im * btr
    row_g = lax.broadcasted_iota(jnp.int32, (SSM_WIDTH, SSM_LANES), 0) // SSM_GROUP
    col_g = lax.broadcasted_iota(jnp.int32, (SSM_WIDTH, SSM_LANES), 1) // SSM_STATE
    diag = row_g == col_g

    def block_diag(small):
        tiled = jnp.concatenate([small] * N_SSM_GROUPS, axis=0)
        return jnp.where(diag, tiled, 0.0).astype(BF16)

    bz_ref[0] = block_diag(bz_re)
    bz_ref[1] = block_diag(bz_im)
    cb_ref[0] = block_diag(ctr_ref[...])
    cb_ref[1] = block_diag(cti_ref[...])


def _ssm_prep(lam_re, lam_im, log_dt, b_re, b_im, c_re, c_im):
    row = lambda t: t.reshape(1, SSM_LANES)
    chan = lambda t: t.reshape(SSM_GROUP, SSM_LANES)
    args = (row(lam_re), row(lam_im), row(jnp.repeat(log_dt, SSM_STATE)),
            chan(b_re.transpose(2, 0, 1)), chan(b_im.transpose(2, 0, 1)),
            chan(c_re.transpose(1, 0, 2)), chan(c_im.transpose(1, 0, 2)))
    return pl.pallas_call(
        _ssm_prep_kernel,
        out_shape=[jax.ShapeDtypeStruct((2, SUBLANES, SSM_LANES), F32),
                   jax.ShapeDtypeStruct((2, SSM_WIDTH, SSM_LANES), BF16),
                   jax.ShapeDtypeStruct((2, SSM_WIDTH, SSM_LANES), BF16)],
        compiler_params=pltpu.CompilerParams(vmem_limit_bytes=VMEM_LIMIT),
        name="ssm_prep",
    )(*args)


_SSM_LANE_CHUNK = 512
_SSM_SLABS = 4


def _ssm_prompt_kernel(u_ref, a_ref, bz_ref, cb_ref, d_ref, z_ref, hre_ref, him_ref, xr, xi, hst, *, tc):
    @pl.when(pl.program_id(1) == 0)
    def _():
        hst[...] = jnp.zeros_like(hst)

    u = u_ref[...]
    ub = u.astype(BF16)
    cw, sw = SSM_WIDTH // _SSM_SLABS, SSM_LANES // _SSM_SLABS
    for m in range(_SSM_SLABS):
        um = ub[:, m * cw:(m + 1) * cw]
        xr[:, m * sw:(m + 1) * sw] = jnp.dot(um, bz_ref[0, m * cw:(m + 1) * cw, m * sw:(m + 1) * sw],
                                             preferred_element_type=F32)
        xi[:, m * sw:(m + 1) * sw] = jnp.dot(um, bz_ref[1, m * cw:(m + 1) * cw, m * sw:(m + 1) * sw],
                                             preferred_element_type=F32)
    for lc in range(SSM_LANES // _SSM_LANE_CHUNK):
        sl = slice(lc * _SSM_LANE_CHUNK, (lc + 1) * _SSM_LANE_CHUNK)
        ar, ai = a_ref[0, :, sl], a_ref[1, :, sl]

        def step(t, carry, sl=sl, ar=ar, ai=ai):
            hr, hi = carry
            rows = pl.ds(pl.multiple_of(t * SUBLANES, SUBLANES), SUBLANES)
            nhr = ar * hr - ai * hi + xr[rows, sl]
            nhi = ar * hi + ai * hr + xi[rows, sl]
            xr[rows, sl] = nhr
            xi[rows, sl] = nhi
            return nhr, nhi

        hr, hi = lax.fori_loop(0, tc, step, (hst[0, :, sl], hst[1, :, sl]))
        hst[0, :, sl] = hr
        hst[1, :, sl] = hi
    ys = []
    for m in range(_SSM_SLABS):
        hm_re = xr[:, m * sw:(m + 1) * sw].astype(BF16)
        hm_im = xi[:, m * sw:(m + 1) * sw].astype(BF16)
        ys.append(lax.dot_general(hm_re, cb_ref[0, m * cw:(m + 1) * cw, m * sw:(m + 1) * sw], _NT,
                                  preferred_element_type=F32)
                  - lax.dot_general(hm_im, cb_ref[1, m * cw:(m + 1) * cw, m * sw:(m + 1) * sw], _NT,
                                    preferred_element_type=F32))
    y = jnp.concatenate(ys, axis=1) + d_ref[...] * u
    z_ref[...] = _gelu(y)
    hre_ref[...] = hst[0]
    him_ref[...] = hst[1]


def _ssm_prompt(u_tm, a, bz, cb, d_row, tc=64):
    G, rows_total, _ = u_tm.shape
    S = rows_total // SUBLANES
    rows = tc * SUBLANES
    full3 = lambda shp: pl.BlockSpec(shp, lambda g, t: (0, 0, 0))
    st_spec = pl.BlockSpec((SUBLANES, SSM_LANES), lambda g, t: (g, 0))
    st_sds = jax.ShapeDtypeStruct((G * SUBLANES, SSM_LANES), F32)
    return pl.pallas_call(
        functools.partial(_ssm_prompt_kernel, tc=tc),
        grid=(G, S // tc),
        in_specs=[pl.BlockSpec((None, rows, SSM_WIDTH), lambda g, t: (g, t, 0)),
                  full3((2, SUBLANES, SSM_LANES)), full3((2, SSM_WIDTH, SSM_LANES)),
                  full3((2, SSM_WIDTH, SSM_LANES)),
                  pl.BlockSpec((1, SSM_WIDTH), lambda g, t: (0, 0))],
        out_specs=[pl.BlockSpec((None, rows, SSM_WIDTH), lambda g, t: (g, t, 0)), st_spec, st_spec],
        out_shape=[jax.ShapeDtypeStruct(u_tm.shape, F32), st_sds, st_sds],
        scratch_shapes=[pltpu.VMEM((rows, SSM_LANES), F32), pltpu.VMEM((rows, SSM_LANES), F32),
                        pltpu.VMEM((2, SUBLANES, SSM_LANES), F32)],
        compiler_params=_cparams("parallel", "arbitrary"),
        name="ssm_prompt",
    )(u_tm, a, bz, cb, d_row)


def _ssm_sample_kernel(u_ref, h0r_ref, h0i_ref, a_ref, bz_ref, cb_ref, d_ref, z_ref, hre_ref, him_ref,
                       xr, xi, *, t_new):
    u = u_ref[...]
    ub = u.astype(BF16)
    nb = h0r_ref.shape[0]
    n_tiles = SSM_LANES // LANES
    x_re = jnp.dot(ub, bz_ref[0], preferred_element_type=F32)
    x_im = jnp.dot(ub, bz_ref[1], preferred_element_type=F32)
    for j in range(n_tiles):
        xr[j] = x_re[:, j * LANES:(j + 1) * LANES]
        xi[j] = x_im[:, j * LANES:(j + 1) * LANES]
    for j in range(n_tiles):
        cs = slice(j * LANES, (j + 1) * LANES)
        ar = jnp.broadcast_to(a_ref[0, 0:1, cs], (nb, LANES))
        ai = jnp.broadcast_to(a_ref[1, 0:1, cs], (nb, LANES))
        hr, hi = h0r_ref[:, cs], h0i_ref[:, cs]
        for t in range(t_new):
            rows = pl.ds(t, nb, stride=t_new)
            nhr = ar * hr - ai * hi + xr[j, rows, :]
            nhi = ar * hi + ai * hr + xi[j, rows, :]
            xr[j, rows, :] = nhr
            xi[j, rows, :] = nhi
            hr, hi = nhr, nhi
        hre_ref[:, cs] = hr
        him_ref[:, cs] = hi
    h_re = jnp.concatenate([xr[j] for j in range(n_tiles)], axis=1).astype(BF16)
    h_im = jnp.concatenate([xi[j] for j in range(n_tiles)], axis=1).astype(BF16)
    y = (lax.dot_general(h_re, cb_ref[0], _NT, preferred_element_type=F32)
         - lax.dot_general(h_im, cb_ref[1], _NT, preferred_element_type=F32)
         + d_ref[...] * u)
    z_ref[...] = _gelu(y)


def _ssm_sample(u, h0r, h0i, a, bz, cb, d_row, t_new):
    n_tok = u.shape[0]
    nb = h0r.shape[0]
    slab = pltpu.VMEM((SSM_LANES // LANES, n_tok, LANES), F32)
    st = jax.ShapeDtypeStruct((nb, SSM_LANES), F32)
    return pl.pallas_call(
        functools.partial(_ssm_sample_kernel, t_new=t_new),
        out_shape=[jax.ShapeDtypeStruct((n_tok, SSM_WIDTH), F32), st, st],
        scratch_shapes=[slab, slab],
        compiler_params=pltpu.CompilerParams(vmem_limit_bytes=VMEM_LIMIT),
        name="ssm_sample",
    )(u, h0r, h0i, a, bz, cb, d_row)


def _tail_kernel(x_ref, att_ref, z_ref, p_ref, wglu_ref, bglu_ref, gatt_ref, gssm_ref, wout_ref, bout_ref,
                 g1_ref, b1_ref, wple_ref, wgate_ref, h_ref, ht_ref, ple_ref):
    z = z_ref[...]
    gate = jnp.dot(z.astype(BF16), wglu_ref[...], preferred_element_type=F32) + bglu_ref[...]
    ssm_out = z * _sigmoid(gate)
    att = att_ref[...]
    rms_a = att * lax.rsqrt(jnp.mean(att * att, axis=-1, keepdims=True) + LN_EPS) * gatt_ref[...]
    rms_s = ssm_out * lax.rsqrt(jnp.mean(ssm_out * ssm_out, axis=-1, keepdims=True) + LN_EPS) * gssm_ref[...]
    mix = (jnp.dot(rms_a.astype(BF16), wout_ref[:ATT_WIDTH, :], preferred_element_type=F32)
           + jnp.dot(rms_s.astype(BF16), wout_ref[ATT_WIDTH:, :], preferred_element_type=F32)
           + bout_ref[...])
    pre = ALPHA * x_ref[...] + mix
    mu = jnp.mean(pre, axis=-1, keepdims=True)
    cen = pre - mu
    var = jnp.mean(cen * cen, axis=-1, keepdims=True)
    h = cen * lax.rsqrt(var + LN_EPS) * g1_ref[...] + b1_ref[...]
    hb = h.astype(BF16)
    h_ref[...] = h
    ht_ref[...] = h.T.astype(BF16)
    gate2 = _sigmoid(jnp.dot(hb, wgate_ref[...], preferred_element_type=F32))
    ple_ref[...] = gate2 * jnp.dot(p_ref[...].astype(BF16), wple_ref[...], preferred_element_type=F32)


def _tail(x3, att3, z_arr, p3, weights, z_time_major):
    B, S, D = x3.shape
    tm = _MXU_COLS
    tok = lambda w: pl.BlockSpec((None, tm, w), lambda b, t: (b, t, 0))
    if z_time_major:
        z_arr = z_arr.reshape(B // SUBLANES, S, SUBLANES * SSM_WIDTH)
        z_spec = pl.BlockSpec((None, tm, SSM_WIDTH), lambda b, t: (b // SUBLANES, t, b % SUBLANES))
    else:
        z_spec = tok(SSM_WIDTH)
    full = lambda a: pl.BlockSpec(a.shape, lambda b, t: (0, 0))
    return pl.pallas_call(
        _tail_kernel,
        grid=(B, S // tm),
        in_specs=[tok(D), tok(ATT_WIDTH), z_spec, tok(PLE_DIM)] + [full(w) for w in weights],
        out_specs=[tok(D), pl.BlockSpec((None, D, tm), lambda b, t: (b * (S // tm) + t, 0, 0)), tok(D)],
        out_shape=[jax.ShapeDtypeStruct((B, S, D), F32), jax.ShapeDtypeStruct((B * S // tm, D, tm), BF16),
                   jax.ShapeDtypeStruct((B, S, D), F32)],
        compiler_params=_cparams("parallel", "parallel"),
        name="tail",
    )(x3, att3, z_arr, p3, *weights)


def _split_bf16(x):
    hi = x.astype(BF16)
    lo = (x - hi.astype(F32)).astype(BF16)
    return hi, lo


def _peer_prep_kernel(wpq_ref, sk1_ref, sk2_ref, m1_ref, m2_ref):
    def combined(sk, w):
        sk_hi, sk_lo = _split_bf16(sk)
        w_hi, w_lo = _split_bf16(w)
        dot = lambda a, b: lax.dot_general(a, b, _NT, preferred_element_type=F32)
        return dot(sk_hi, w_hi) + (dot(sk_hi, w_lo) + dot(sk_lo, w_hi))

    half = D_KEY // 2
    m1_ref[...] = combined(sk1_ref[...], wpq_ref[:, :half]).astype(BF16)
    m2_ref[...] = combined(sk2_ref[...], wpq_ref[:, half:]).astype(BF16)


def _peer_prep(w_pq, sk1, sk2):
    half = D_KEY // 2
    m1, m2 = pl.pallas_call(
        _peer_prep_kernel,
        grid=(PEER_HEADS,),
        in_specs=[pl.BlockSpec((D_MODEL, D_KEY), lambda h: (0, h)),
                  pl.BlockSpec((N_KEYS, half), lambda h: (0, 0)),
                  pl.BlockSpec((N_KEYS, half), lambda h: (0, 0))],
        out_specs=[pl.BlockSpec((N_KEYS, D_MODEL), lambda h: (0, h)),
                   pl.BlockSpec((None, N_KEYS, D_MODEL), lambda h: (h, 0, 0))],
        out_shape=[jax.ShapeDtypeStruct((N_KEYS, PEER_HEADS * D_MODEL), BF16),
                   jax.ShapeDtypeStruct((PEER_HEADS, N_KEYS, D_MODEL), BF16)],
        compiler_params=_cparams("parallel"),
        name="peer_prep",
    )(w_pq, sk1, sk2)
    return jnp.concatenate([m1.reshape(N_KEYS * PEER_HEADS, D_MODEL),
                            m2.reshape(PEER_HEADS * N_KEYS, D_MODEL)], axis=0)


def _vmax(a, b):
    return b if a is None else a if b is None else jnp.maximum(a, b)


def _vmin(a, b):
    return None if a is None or b is None else jnp.minimum(a, b)


def _larger_first(xs, i, l):
    a, b = xs[i], xs[l]
    xs[i], xs[l] = _vmax(a, b), _vmin(a, b)


def _sort_desc(xs):
    xs = list(xs)
    n, k = len(xs), 2
    while k <= n:
        j = k // 2
        while j >= 1:
            for i in range(n):
                l = i ^ j
                if l > i:
                    if (i & k) == 0:
                        _larger_first(xs, i, l)
                    else:
                        _larger_first(xs, l, i)
            j //= 2
        k *= 2
    return xs


def _merge_top(a, b):
    n = len(a)
    c = [_vmax(a[i], b[n - 1 - i]) for i in range(n)]
    j = n // 2
    while j >= 1:
        for i in range(n):
            l = i ^ j
            if l > i:
                _larger_first(c, i, l)
        j //= 2
    return c


def _top_of_lists(lists):
    lists = [l + [None] * (PEER_TOPK - len(l)) for l in lists]
    while len(lists) > 1:
        nxt = [_merge_top(lists[i], lists[i + 1]) for i in range(0, len(lists) - 1, 2)]
        if len(lists) % 2:
            nxt.append(lists[-1])
        lists = nxt
    return lists[0]


_CAND_PAIRS = [(a, b) for a in range(PEER_TOPK) for b in range(PEER_TOPK) if (a + 1) * (b + 1) <= PEER_TOPK]
_CAND_SOLO_ROWS = 5


def _peer_gate_kernel(ht_ref, mt_ref, e1_ref, e2_ref, tau_ref, *, tn):
    st = jnp.dot(mt_ref[...], ht_ref[...], preferred_element_type=F32)
    n1 = N_KEYS * PEER_HEADS
    rowid = lax.broadcasted_iota(jnp.int32, (SUBLANES, LANES), 0)
    for jt in range(tn // LANES):
        cs = slice(jt * LANES, (jt + 1) * LANES)
        s1 = [st[i * PEER_HEADS:(i + 1) * PEER_HEADS, cs] for i in range(N_KEYS)]
        top1 = _top_of_lists([_sort_desc(s1[g:g + PEER_TOPK]) for g in range(0, N_KEYS, PEER_TOPK)])
        max1 = top1[0]
        for i in range(N_KEYS):
            e1_ref[jt, i * PEER_HEADS:(i + 1) * PEER_HEADS, :] = jnp.exp(s1[i] - max1)
        v1 = [jnp.exp(t - max1) for t in top1]
        e2_heads, v2_heads = [], []
        for h in range(PEER_HEADS):
            s2 = st[n1 + h * N_KEYS:n1 + (h + 1) * N_KEYS, cs]
            top = _sort_desc([s2[r * SUBLANES:(r + 1) * SUBLANES] for r in range(N_KEYS // SUBLANES)])
            for shift in (4, 2, 1):
                top = _merge_top(top, [pltpu.roll(t, shift, 0) for t in top])
            max2 = top[0]
            e2_heads.append(jnp.exp(s2 - max2[0:1, :]))
            v2_heads.append([jnp.exp(t - max2) for t in top])
        v2 = []
        for b in range(PEER_TOPK):
            acc = v2_heads[0][b]
            for h in range(1, PEER_HEADS):
                acc = jnp.where(rowid == h, v2_heads[h][b], acc)
            v2.append(acc)
        cand0 = {(a, b): v1[a] * v2[b] for a, b in _CAND_PAIRS}
        lists = [[cand0[(a, b)] for b in range(PEER_TOPK) if (a, b) in cand0] for a in range(_CAND_SOLO_ROWS)]
        lists += [[cand0[(a, b)] for a in range(_CAND_SOLO_ROWS, PEER_TOPK) if (a, b) in cand0]
                  for b in range(PEER_TOPK // (_CAND_SOLO_ROWS + 1))]
        assert sum(len(l) for l in lists) == len(_CAND_PAIRS)
        top = _top_of_lists(lists)
        tau0 = top[PEER_TOPK - 1]
        rz = 1.0 / functools.reduce(lambda x, y: x + y, top)
        v2n = [v * rz for v in v2]
        tau = None
        for (a, b), c0 in cand0.items():
            c = jnp.where(c0 >= tau0, v1[a] * v2n[b], jnp.inf)
            tau = c if tau is None else jnp.minimum(tau, c)
        tau_ref[jt] = tau
        for h in range(PEER_HEADS):
            e2_ref[jt, h * N_KEYS:(h + 1) * N_KEYS, :] = e2_heads[h] * rz[h:h + 1, :]


def _peer_gate(ht, mt):
    tn = _MXU_COLS
    n_tok = ht.shape[0] * tn
    nt = tn // LANES
    slab = lambda rows: pl.BlockSpec((nt, rows, LANES), lambda t: (t, 0, 0))
    return pl.pallas_call(
        functools.partial(_peer_gate_kernel, tn=tn),
        grid=(n_tok // tn,),
        in_specs=[pl.BlockSpec((None, D_MODEL, tn), lambda t: (t, 0, 0)),
                  pl.BlockSpec(mt.shape, lambda t: (0, 0))],
        out_specs=[slab(N_KEYS * PEER_HEADS), slab(N_KEYS * PEER_HEADS), slab(SUBLANES)],
        out_shape=[jax.ShapeDtypeStruct((n_tok // LANES, N_KEYS * PEER_HEADS, LANES), F32),
                   jax.ShapeDtypeStruct((n_tok // LANES, N_KEYS * PEER_HEADS, LANES), F32),
                   jax.ShapeDtypeStruct((n_tok // LANES, SUBLANES, LANES), F32)],
        compiler_params=_cparams("parallel"),
        name="peer_gate",
    )(ht, mt)


def _gelu_times(x, g):
    c = math.sqrt(2.0 / math.pi)
    hx = 0.5 * x
    t = jnp.tanh(x * (x * x * (c * 0.044715) + c))
    return g * (hx + hx * t)


_KEYS_PER_SLICE = 1


def _peer_dense_kernel(ht_ref, u_ref, vt_ref, e1_ref, e2_ref, tau_ref, o_ref, acc_t, w_even, w_odd, *,
                       tb, ec, n_chunks):
    s = pl.program_id(1)
    n_pairs = tb // _MXU_COLS
    keys_per_chunk = ec // N_KEYS
    slices_per_pair = keys_per_chunk // _KEYS_PER_SLICE
    n_iter = n_pairs * slices_per_pair
    e_rows = _KEYS_PER_SLICE * N_KEYS
    v_rows = D_MODEL // slices_per_pair
    chunk = jnp.minimum(s, n_chunks - 1)

    @pl.when(s == 0)
    def _():
        acc_t[...] = jnp.zeros_like(acc_t)
        w_odd[...] = jnp.zeros_like(w_odd)

    def scores(it):
        pr, sl = it // slices_per_pair, it % slices_per_pair
        return jnp.dot(u_ref[sl * e_rows:(sl + 1) * e_rows, :], ht_ref[pr], preferred_element_type=F32)

    def step(w_in, w_out):
        a_next = scores(0)
        for it in range(n_iter):
            pr, sl = it // slices_per_pair, it % slices_per_pair
            a = a_next
            if it + 1 < n_iter:
                a_next = scores(it + 1)
            vrows = slice(sl * v_rows, (sl + 1) * v_rows)
            acc_t[pr, vrows, :] += jnp.dot(vt_ref[vrows, :], w_in[pr], preferred_element_type=F32)
            halves = []
            for half in range(_MXU_COLS // LANES):
                j = pr * (_MXU_COLS // LANES) + half
                cs = slice(half * LANES, (half + 1) * LANES)
                taub = [jnp.broadcast_to(tau_ref[j, pl.ds(h, 1), :], (SUBLANES, LANES)) for h in range(PEER_HEADS)]
                pieces = []
                for kk in range(_KEYS_PER_SLICE):
                    k1 = chunk * keys_per_chunk + sl * _KEYS_PER_SLICE + kk
                    e1b = [jnp.broadcast_to(e1_ref[j, pl.ds(k1 * PEER_HEADS + h, 1), :], (SUBLANES, LANES))
                           for h in range(PEER_HEADS)]
                    for kb in range(N_KEYS // SUBLANES):
                        gate = None
                        for h in range(PEER_HEADS):
                            prod = e1b[h] * e2_ref[j, pl.ds(h * N_KEYS + kb * SUBLANES, SUBLANES), :]
                            sel = jnp.where(prod >= taub[h], prod, 0.0)
                            gate = sel if gate is None else gate + sel
                        r0 = kk * N_KEYS + kb * SUBLANES
                        pieces.append(_gelu_times(a[r0:r0 + SUBLANES, cs], gate))
                halves.append(jnp.concatenate(pieces, axis=0))
            w_out[pr, sl * e_rows:(sl + 1) * e_rows, :] = jnp.concatenate(halves, axis=1).astype(BF16)

    @pl.when(s % 2 == 0)
    def _():
        step(w_odd, w_even)

    @pl.when(s % 2 == 1)
    def _():
        step(w_even, w_odd)

    @pl.when(s == n_chunks)
    def _():
        for pr in range(n_pairs):
            o_ref[pr * _MXU_COLS:(pr + 1) * _MXU_COLS, :] = acc_t[pr].T


def _peer_dense(ht, u_bf, vt_bf, e1, e2, tau, tb=1024, ec=512):
    n_tok = ht.shape[0] * _MXU_COLS
    tb = min(tb, n_tok)
    nt = tb // LANES
    n_chunks = N_EXPERTS // ec
    slab = lambda rows: pl.BlockSpec((nt, rows, LANES), lambda t, s: (t, 0, 0))
    w_buf = pltpu.VMEM((tb // _MXU_COLS, ec, _MXU_COLS), BF16)
    return pl.pallas_call(
        functools.partial(_peer_dense_kernel, tb=tb, ec=ec, n_chunks=n_chunks),
        grid=(n_tok // tb, n_chunks + 1),
        in_specs=[pl.BlockSpec((tb // _MXU_COLS, D_MODEL, _MXU_COLS), lambda t, s: (t, 0, 0)),
                  pl.BlockSpec((ec, D_MODEL), lambda t, s: (jnp.minimum(s, n_chunks - 1), 0)),
                  pl.BlockSpec((D_MODEL, ec), lambda t, s: (0, jnp.maximum(s - 1, 0))),
                  slab(N_KEYS * PEER_HEADS), slab(N_KEYS * PEER_HEADS), slab(SUBLANES)],
        out_specs=pl.BlockSpec((tb, D_MODEL), lambda t, s: (t, 0)),
        out_shape=jax.ShapeDtypeStruct((n_tok, D_MODEL), F32),
        scratch_shapes=[pltpu.VMEM((tb // _MXU_COLS, D_MODEL, _MXU_COLS), F32), w_buf, w_buf],
        compiler_params=_cparams("parallel", "arbitrary"),
        name="peer_dense",
    )(ht, u_bf, vt_bf, e1, e2, tau)


def _final_kernel(h_ref, f_ref, p_ref, g_ref, b_ref, o_ref):
    pre = ALPHA * h_ref[...] + f_ref[...] + p_ref[...]
    mu = jnp.mean(pre, axis=-1, keepdims=True)
    cen = pre - mu
    var = jnp.mean(cen * cen, axis=-1, keepdims=True)
    o_ref[...] = cen * lax.rsqrt(var + LN_EPS) * g_ref[...] + b_ref[...]


def _final(h2, ffn2, ple2, g, b, tm=512):
    n_tok, D = h2.shape
    tm = min(tm, n_tok)
    tok = pl.BlockSpec((tm, D), lambda t: (t, 0))
    vec = pl.BlockSpec((1, D), lambda t: (0, 0))
    return pl.pallas_call(
        _final_kernel,
        grid=(n_tok // tm,),
        in_specs=[tok, tok, tok, vec, vec],
        out_specs=tok,
        out_shape=jax.ShapeDtypeStruct((n_tok, D), F32),
        compiler_params=_cparams("parallel"),
        name="final_norm",
    )(h2, ffn2, ple2, g, b)


def _ffn_and_norm(h3, ht, ple3, mt, u_bf, vt_bf, ln2_g, ln2_b):
    B, S, D = h3.shape
    n_tok = B * S
    e1, e2, tau = _peer_gate(ht, mt)
    ffn = _peer_dense(ht, u_bf, vt_bf, e1, e2, tau)
    y = _final(h3.reshape(n_tok, D), ffn, ple3.reshape(n_tok, D), ln2_g, ln2_b)
    return y.reshape(B, S, D)


def kernel(x_prompt, x_sample, cache_k, cache_v, state_ssm_re, state_ssm_im, p_prompt, p_sample,
           w_in, b_in, lam_re, lam_im, log_dt, ssm_b_re, ssm_b_im, ssm_c_re, ssm_c_im, ssm_d,
           w_glu, b_glu, g_att_out, g_ssm_out, w_out, b_out, ln1_g, ln1_b,
           w_pq, sub_keys1, sub_keys2, peer_u, peer_v, w_ple, w_ple_gate, ln2_g, ln2_b):
    assert w_in.shape[0] == DEPTH == 1
    B, S, D = x_prompt.shape
    Bs, Ts, _ = x_sample.shape
    row = lambda t: t.reshape(1, -1)

    w_in_bf = w_in[0].astype(BF16)
    b_in2 = row(b_in[0])
    a, bz, cb = _ssm_prep(lam_re[0], lam_im[0], log_dt[0], ssm_b_re[0], ssm_b_im[0], ssm_c_re[0], ssm_c_im[0])
    d_row = row(ssm_d[0])
    tail_w = (w_glu[0].astype(BF16), row(b_glu[0]), row(g_att_out[0]), row(g_ssm_out[0]),
              w_out[0].astype(BF16), row(b_out[0]), row(ln1_g[0]), row(ln1_b[0]),
              w_ple[0].astype(BF16), w_ple_gate[0].astype(BF16))
    mt = _peer_prep(w_pq[0], sub_keys1[0], sub_keys2[0])
    u_bf = peer_u[0].astype(BF16)
    vt_bf = peer_v[0].astype(BF16).T
    g2, b2 = row(ln2_g[0]), row(ln2_b[0])

    q, k_p, v_p, u_tm, kt_p, vt_p = _in_proj(x_prompt, w_in_bf, b_in2, prompt=True)
    att = _attn_prompt(q, k_p, v_p)
    z_tm, hre_p, him_p = _ssm_prompt(u_tm.reshape(B // SUBLANES, S * SUBLANES, SSM_WIDTH), a, bz, cb, d_row)
    h3, hb3, ple3 = _tail(x_prompt, att, z_tm, p_prompt[0], tail_w, z_time_major=True)
    y_p = _ffn_and_norm(h3, hb3, ple3, mt, u_bf, vt_bf, g2, b2)

    n_s = Bs * Ts
    q_s, k_s, v_s, u_s = _in_proj(x_sample.reshape(1, n_s, D), w_in_bf, b_in2, prompt=False)
    per_seq = lambda t: t.reshape(Bs, Ts, ATT_WIDTH)
    att_s = _attn_sample(per_seq(q_s), per_seq(k_s), per_seq(v_s), cache_k[0], cache_v[0])
    z_s, hre_s, him_s = _ssm_sample(u_s.reshape(n_s, SSM_WIDTH), state_ssm_re[0].reshape(Bs, SSM_LANES),
                                    state_ssm_im[0].reshape(Bs, SSM_LANES), a, bz, cb, d_row, Ts)
    h3s, hb3s, ple3s = _tail(x_sample.reshape(1, n_s, D), att_s.reshape(1, n_s, ATT_WIDTH),
                             z_s.reshape(1, n_s, SSM_WIDTH), p_sample[0].reshape(1, n_s, PLE_DIM), tail_w,
                             z_time_major=False)
    y_s = _ffn_and_norm(h3s, hb3s, ple3s, mt, u_bf, vt_bf, g2, b2).reshape(Bs, Ts, D)

    w_keep = min(BRANCHES[-1][0], S)
    heads = lambda t, b, s: t.reshape(1, b, s, N_HEADS, HEAD_DIM)
    window = lambda t: (t.reshape(B, N_HEADS, HEAD_DIM, S)[..., S - w_keep:].transpose(0, 3, 1, 2)
                        .reshape(1, B, w_keep, N_HEADS, HEAD_DIM))
    state = lambda t, b: t.reshape(1, b, N_SSM_GROUPS, SSM_STATE)
    return (y_p, y_s,
            window(kt_p), window(vt_p),
            state(hre_p, B), state(him_p, B),
            heads(k_s, Bs, Ts), heads(v_s, Bs, Ts),
            state(hre_s, Bs), state(him_s, Bs))
```

```python
import functools
import math

import jax
import jax.numpy as jnp
import numpy as np
from jax import lax
from jax.experimental import pallas as pl
from jax.experimental.pallas import tpu as pltpu

D_MODEL = 1024
ATT_WIDTH = 512
SSM_WIDTH = 512
HEAD_DIM = 64
N_HEADS = 8
BRANCHES = ((128, 1), (512, 4), (2048, 16))
BLK = 128
SSM_GROUP = 16
N_SSM_GROUPS = 32
SSM_STATE = 64
SSM_LANES = N_SSM_GROUPS * SSM_STATE
PEER_HEADS = 8
N_KEYS = 128
N_EXPERTS = N_KEYS * N_KEYS
PEER_TOPK = 16
D_KEY = 256
PLE_DIM = 256
DEPTH = 1
ALPHA = (2 * DEPTH) ** 0.25
LN_EPS = 1e-5

LANES = 128
SUBLANES = 8
_MXU_COLS = 2 * LANES
NEG_BIG = -1e30
VMEM_LIMIT = 56 * 1024 * 1024

F32 = jnp.float32
BF16 = jnp.bfloat16

_NT = (((1,), (1,)), ((), ()))


def _cparams(*sem):
    return pltpu.CompilerParams(dimension_semantics=sem, vmem_limit_bytes=VMEM_LIMIT)


def _gelu(x):
    c = math.sqrt(2.0 / math.pi)
    return 0.5 * x * (1.0 + jnp.tanh(c * (x + 0.044715 * (x * x * x))))


def _sigmoid(x):
    return 1.0 / (1.0 + jnp.exp(-x))


def _in_proj_kernel(x_ref, w_ref, b_ref, q_ref, k_ref, v_ref, u_ref, *kv_t_refs):
    x = x_ref[...].astype(BF16)
    proj = jnp.dot(x, w_ref[...], preferred_element_type=F32) + b_ref[...]
    q_ref[...] = proj[:, :ATT_WIDTH] * (HEAD_DIM ** -0.5)
    k = proj[:, ATT_WIDTH:2 * ATT_WIDTH]
    v = proj[:, 2 * ATT_WIDTH:3 * ATT_WIDTH]
    k_ref[...] = k
    v_ref[...] = v
    u_ref[...] = proj[:, 3 * ATT_WIDTH:]
    if kv_t_refs:
        kv_t_refs[0][...] = k.T
        kv_t_refs[1][...] = v.T


def _in_proj(x3, w_bf, b2, prompt, tm=512):
    B, S, D = x3.shape
    n_out = w_bf.shape[1]
    tm = min(tm, S)
    spec = pl.BlockSpec((None, tm, ATT_WIDTH), lambda b, t: (b, t, 0))
    sds = jax.ShapeDtypeStruct((B, S, ATT_WIDTH), F32)
    out_specs, out_shape = [spec, spec, spec], [sds, sds, sds]
    if prompt:
        out_specs.append(pl.BlockSpec((None, tm, SSM_WIDTH), lambda b, t: (b // SUBLANES, t, b % SUBLANES)))
        out_shape.append(jax.ShapeDtypeStruct((B // SUBLANES, S, SUBLANES * SSM_WIDTH), F32))
        spec_t = pl.BlockSpec((None, ATT_WIDTH, tm), lambda b, t: (b, 0, t))
        out_specs += [spec_t, spec_t]
        out_shape += [jax.ShapeDtypeStruct((B, ATT_WIDTH, S), F32)] * 2
    else:
        out_specs.append(spec)
        out_shape.append(jax.ShapeDtypeStruct((B, S, SSM_WIDTH), F32))
    return pl.pallas_call(
        _in_proj_kernel,
        grid=(B, S // tm),
        in_specs=[pl.BlockSpec((None, tm, D), lambda b, t: (b, t, 0)),
                  pl.BlockSpec((D, n_out), lambda b, t: (0, 0)),
                  pl.BlockSpec((1, n_out), lambda b, t: (0, 0))],
        out_specs=out_specs,
        out_shape=out_shape,
        compiler_params=_cparams("parallel", "parallel"),
        name="in_proj",
    )(x3, w_bf, b2)


def _alibi_slopes():
    return [2.0 ** (-8.0 * (h + 1) / N_HEADS) for h in range(N_HEADS)]


def _prompt_bias_table():
    slopes = jnp.asarray(_alibi_slopes(), F32).reshape(N_HEADS // 2, 1, 2, 1, 1)
    qi = jnp.arange(BLK)[:, None]
    kj = jnp.arange(2 * BLK)[None, :]
    delta = qi - kj + BLK
    tabs = []
    for window, dil in BRANCHES:
        steps = window // dil
        valid = (delta >= 0) & (delta <= steps)
        dist = (dil * delta).astype(F32)
        tabs.append(jnp.where(valid, -slopes * dist, NEG_BIG))
    return jnp.concatenate(tabs, axis=1)


_ATTN_UNROLL = 8


def _attn_prompt_kernel(q_ref, k_ref, v_ref, bias_ref, o_ref, acc_scr, m_scr, l_scr, *, seq):
    lane = lax.broadcasted_iota(jnp.int32, (BLK, LANES), 1)
    is_a = lane < HEAD_DIM

    for br, (_, dil) in enumerate(BRANCHES):
        nb = (seq // dil) // BLK

        def block_step(idx, carry, br=br, dil=dil, nb=nb):
            r = idx // nb
            n = idx % nb
            start = n * (BLK * dil) + r
            pstart = jnp.maximum(n - 1, 0) * (BLK * dil) + r
            if dil == 1:
                rows = pl.ds(pl.multiple_of(start, BLK), BLK)
                prows = pl.ds(pl.multiple_of(pstart, BLK), BLK)
            else:
                rows = pl.ds(start, BLK, stride=dil)
                prows = pl.ds(pstart, BLK, stride=dil)
            qb = q_ref[rows, :]
            kk = jnp.concatenate([k_ref[prows, :], k_ref[rows, :]], axis=0).astype(BF16)
            vv = jnp.concatenate([v_ref[prows, :], v_ref[rows, :]], axis=0).astype(BF16)
            pen = jnp.where(n == 0, NEG_BIG, 0.0).astype(F32)
            pen_tile = jnp.concatenate([jnp.full((BLK, BLK), pen, F32), jnp.zeros((BLK, BLK), F32)], axis=1)
            outs = []
            for hh in range(2):
                qh = jnp.where(is_a if hh == 0 else jnp.logical_not(is_a), qb, 0.0).astype(BF16)
                s = lax.dot_general(qh, kk, _NT, preferred_element_type=F32)
                s = s + bias_ref[br, hh] + pen_tile
                m = jnp.max(s, axis=-1, keepdims=True)
                p = jnp.exp(s - m)
                l = jnp.sum(p, axis=-1, keepdims=True)
                acc = jnp.dot(p.astype(BF16), vv, preferred_element_type=F32)
                outs.append((acc, m, l))
            (acc_a, m_a, l_a), (acc_b, m_b, l_b) = outs
            acc_scr[br, rows, :] = jnp.where(is_a, acc_a, acc_b)
            m_scr[br, rows, :] = jnp.where(is_a, m_a, m_b)
            l_scr[br, rows, :] = jnp.where(is_a, l_a, l_b)
            return carry

        lax.fori_loop(0, dil * nb, block_step, 0, unroll=_ATTN_UNROLL)

    chunk = 256
    for c0 in range(0, seq, chunk):
        sl = pl.ds(c0, chunk)
        m1, m2, m3 = m_scr[0, sl, :], m_scr[1, sl, :], m_scr[2, sl, :]
        mm = jnp.maximum(jnp.maximum(m1, m2), m3)
        w1, w2, w3 = jnp.exp(m1 - mm), jnp.exp(m2 - mm), jnp.exp(m3 - mm)
        num = w1 * acc_scr[0, sl, :] + w2 * acc_scr[1, sl, :] + w3 * acc_scr[2, sl, :]
        den = w1 * l_scr[0, sl, :] + w2 * l_scr[1, sl, :] + w3 * l_scr[2, sl, :]
        o_ref[sl, :] = num / den


def _attn_prompt(q, k, v):
    B, S, _ = q.shape
    assert S == BRANCHES[-1][1] * BLK, "dilated blocks assume SEQ == 16 * 128"
    bias = _prompt_bias_table()
    spec = pl.BlockSpec((None, S, LANES), lambda b, hp: (b, 0, hp))
    scr = pltpu.VMEM((len(BRANCHES), S, LANES), F32)
    return pl.pallas_call(
        functools.partial(_attn_prompt_kernel, seq=S),
        grid=(B, ATT_WIDTH // LANES),
        in_specs=[spec, spec, spec,
                  pl.BlockSpec((None, len(BRANCHES), 2, BLK, 2 * BLK), lambda b, hp: (hp, 0, 0, 0, 0))],
        out_specs=spec,
        out_shape=jax.ShapeDtypeStruct((B, S, ATT_WIDTH), F32),
        scratch_shapes=[scr, scr, scr],
        compiler_params=_cparams("parallel", "parallel"),
        name="attn_prompt",
    )(q, k, v, bias)


def _sample_bias_tables(w_buf, t_new):
    slopes = np.asarray(_alibi_slopes(), np.float64)[:, None, None]
    t = np.arange(t_new)[None, :, None]

    def tables(key_pos):
        delta = (w_buf + t) - key_pos[None, None, :]
        out = []
        for window, dil in BRANCHES:
            ok = (delta >= 0) & (delta <= window) & (delta % dil == 0)
            out.append(np.where(ok, -slopes * delta, NEG_BIG))
        return jnp.asarray(np.stack(out).astype(np.float32))

    return tables(np.arange(w_buf)), tables(w_buf + np.arange(t_new))


def _attn_sample_kernel(q_ref, kn_ref, vn_ref, kt_ref, vt_ref, bias_ref, biasn_ref, o_ref):
    outs = []
    for h in range(N_HEADS):
        hs = slice(h * HEAD_DIM, (h + 1) * HEAD_DIM)
        qh = q_ref[:, hs].astype(BF16)
        knh = kn_ref[:, hs].astype(BF16)
        vnh = vn_ref[:, hs].astype(BF16)
        s_c = jnp.dot(qh, kt_ref[h].astype(BF16), preferred_element_type=F32)
        s_n = lax.dot_general(qh, knh, _NT, preferred_element_type=F32)
        sb_c = [s_c + bias_ref[r, h] for r in range(len(BRANCHES))]
        sb_n = [s_n + biasn_ref[r, h] for r in range(len(BRANCHES))]
        mm = None
        for a in sb_c + sb_n:
            am = jnp.max(a, axis=-1, keepdims=True)
            mm = am if mm is None else jnp.maximum(mm, am)
        p_c = functools.reduce(lambda x, y: x + y, [jnp.exp(a - mm) for a in sb_c])
        p_n = functools.reduce(lambda x, y: x + y, [jnp.exp(a - mm) for a in sb_n])
        den = jnp.sum(p_c, axis=-1, keepdims=True) + jnp.sum(p_n, axis=-1, keepdims=True)
        acc = (lax.dot_general(p_c.astype(BF16), vt_ref[h].astype(BF16), _NT, preferred_element_type=F32)
               + jnp.dot(p_n.astype(BF16), vnh, preferred_element_type=F32))
        outs.append(acc / den)
    o_ref[...] = jnp.concatenate(outs, axis=1)


def _attn_sample(q, kn, vn, cache_k, cache_v):
    B, T, _ = q.shape
    W = cache_k.shape[1]
    assert W >= BRANCHES[-1][0], "every dilated key of a sample query must lie inside the window buffer"
    bias, bias_new = _sample_bias_tables(W, T)
    new = pl.BlockSpec((None, T, ATT_WIDTH), lambda b: (b, 0, 0))
    buf = pl.BlockSpec((None, N_HEADS, HEAD_DIM, W), lambda b: (b, 0, 0, 0))
    const = lambda a: pl.BlockSpec(a.shape, lambda b: (0, 0, 0, 0))
    return pl.pallas_call(
        _attn_sample_kernel,
        grid=(B,),
        in_specs=[new, new, new, buf, buf, const(bias), const(bias_new)],
        out_specs=new,
        out_shape=jax.ShapeDtypeStruct((B, T, ATT_WIDTH), F32),
        compiler_params=_cparams("parallel"),
        name="attn_sample",
    )(q, kn, vn, cache_k.transpose(0, 2, 3, 1), cache_v.transpose(0, 2, 3, 1), bias, bias_new)


def _ssm_prep_kernel(lr_ref, li_ref, ldt_ref, btr_ref, bti_ref, ctr_ref, cti_ref, a_ref, bz_ref, cb_ref):
    lr, li = lr_ref[...], li_ref[...]
    dt = jnp.exp(ldt_ref[...])
    mag = jnp.exp(lr * dt)
    a_re = mag * jnp.cos(li * dt)
    a_im = mag * jnp.sin(li * dt)
    den = lr * lr + li * li
    nr = a_re - 1.0
    z_re = (nr * lr + a_im * li) / den
    z_im = (a_im * lr - nr * li) / den
    a_ref[0] = jnp.broadcast_to(a_re, (SUBLANES, SSM_LANES))
    a_ref[1] = jnp.broadcast_to(a_im, (SUBLANES, SSM_LANES))
    btr, bti = btr_ref[...], bti_ref[...]
    bz_re = z_re * btr - z_im * bti
    bz_im = z_re * bti + z_im * btr
    row_g = lax.broadcasted_iota(jnp.int32, (SSM_WIDTH, SSM_LANES), 0) // SSM_GROUP
    col_g = lax.broadcasted_iota(jnp.int32, (SSM_WIDTH, SSM_LANES), 1) // SSM_STATE
    diag = row_g == col_g

    def block_diag(small):
        tiled = jnp.concatenate([small] * N_SSM_GROUPS, axis=0)
        return jnp.where(diag, tiled, 0.0).astype(BF16)

    bz_ref[0] = block_diag(bz_re)
    bz_ref[1] = block_diag(bz_im)
    cb_ref[0] = block_diag(ctr_ref[...])
    cb_ref[1] = block_diag(cti_ref[...])


def _ssm_prep(lam_re, lam_im, log_dt, b_re, b_im, c_re, c_im):
    row = lambda t: t.reshape(1, SSM_LANES)
    chan = lambda t: t.reshape(SSM_GROUP, SSM_LANES)
    args = (row(lam_re), row(lam_im), row(jnp.repeat(log_dt, SSM_STATE)),
            chan(b_re.transpose(2, 0, 1)), chan(b_im.transpose(2, 0, 1)),
            chan(c_re.transpose(1, 0, 2)), chan(c_im.transpose(1, 0, 2)))
    return pl.pallas_call(
        _ssm_prep_kernel,
        out_shape=[jax.ShapeDtypeStruct((2, SUBLANES, SSM_LANES), F32),
                   jax.ShapeDtypeStruct((2, SSM_WIDTH, SSM_LANES), BF16),
                   jax.ShapeDtypeStruct((2, SSM_WIDTH, SSM_LANES), BF16)],
        compiler_params=pltpu.CompilerParams(vmem_limit_bytes=VMEM_LIMIT),
        name="ssm_prep",
    )(*args)


_SSM_LANE_CHUNK = 512
_SSM_SLABS = 4


def _ssm_prompt_kernel(u_ref, a_ref, bz_ref, cb_ref, d_ref, z_ref, hre_ref, him_ref, xr, xi, hst, *, tc):
    @pl.when(pl.program_id(1) == 0)
    def _():
        hst[...] = jnp.zeros_like(hst)

    u = u_ref[...]
    ub = u.astype(BF16)
    cw, sw = SSM_WIDTH // _SSM_SLABS, SSM_LANES // _SSM_SLABS
    for m in range(_SSM_SLABS):
        um = ub[:, m * cw:(m + 1) * cw]
        xr[:, m * sw:(m + 1) * sw] = jnp.dot(um, bz_ref[0, m * cw:(m + 1) * cw, m * sw:(m + 1) * sw],
                                             preferred_element_type=F32)
        xi[:, m * sw:(m + 1) * sw] = jnp.dot(um, bz_ref[1, m * cw:(m + 1) * cw, m * sw:(m + 1) * sw],
                                             preferred_element_type=F32)
    for lc in range(SSM_LANES // _SSM_LANE_CHUNK):
        sl = slice(lc * _SSM_LANE_CHUNK, (lc + 1) * _SSM_LANE_CHUNK)
        ar, ai = a_ref[0, :, sl], a_ref[1, :, sl]

        def step(t, carry, sl=sl, ar=ar, ai=ai):
            hr, hi = carry
            rows = pl.ds(pl.multiple_of(t * SUBLANES, SUBLANES), SUBLANES)
            nhr = ar * hr - ai * hi + xr[rows, sl]
            nhi = ar * hi + ai * hr + xi[rows, sl]
            xr[rows, sl] = nhr
            xi[rows, sl] = nhi
            return nhr, nhi

        hr, hi = lax.fori_loop(0, tc, step, (hst[0, :, sl], hst[1, :, sl]))
        hst[0, :, sl] = hr
        hst[1, :, sl] = hi
    ys = []
    for m in range(_SSM_SLABS):
        hm_re = xr[:, m * sw:(m + 1) * sw].astype(BF16)
        hm_im = xi[:, m * sw:(m + 1) * sw].astype(BF16)
        ys.append(lax.dot_general(hm_re, cb_ref[0, m * cw:(m + 1) * cw, m * sw:(m + 1) * sw], _NT,
                                  preferred_element_type=F32)
                  - lax.dot_general(hm_im, cb_ref[1, m * cw:(m + 1) * cw, m * sw:(m + 1) * sw], _NT,
                                    preferred_element_type=F32))
    y = jnp.concatenate(ys, axis=1) + d_ref[...] * u
    z_ref[...] = _gelu(y)
    hre_ref[...] = hst[0]
    him_ref[...] = hst[1]


def _ssm_prompt(u_tm, a, bz, cb, d_row, tc=64):
    G, rows_total, _ = u_tm.shape
    S = rows_total // SUBLANES
    rows = tc * SUBLANES
    full3 = lambda shp: pl.BlockSpec(shp, lambda g, t: (0, 0, 0))
    st_spec = pl.BlockSpec((SUBLANES, SSM_LANES), lambda g, t: (g, 0))
    st_sds = jax.ShapeDtypeStruct((G * SUBLANES, SSM_LANES), F32)
    return pl.pallas_call(
        functools.partial(_ssm_prompt_kernel, tc=tc),
        grid=(G, S // tc),
        in_specs=[pl.BlockSpec((None, rows, SSM_WIDTH), lambda g, t: (g, t, 0)),
                  full3((2, SUBLANES, SSM_LANES)), full3((2, SSM_WIDTH, SSM_LANES)),
                  full3((2, SSM_WIDTH, SSM_LANES)),
                  pl.BlockSpec((1, SSM_WIDTH), lambda g, t: (0, 0))],
        out_specs=[pl.BlockSpec((None, rows, SSM_WIDTH), lambda g, t: (g, t, 0)), st_spec, st_spec],
        out_shape=[jax.ShapeDtypeStruct(u_tm.shape, F32), st_sds, st_sds],
        scratch_shapes=[pltpu.VMEM((rows, SSM_LANES), F32), pltpu.VMEM((rows, SSM_LANES), F32),
                        pltpu.VMEM((2, SUBLANES, SSM_LANES), F32)],
        compiler_params=_cparams("parallel", "arbitrary"),
        name="ssm_prompt",
    )(u_tm, a, bz, cb, d_row)


def _ssm_sample_kernel(u_ref, h0r_ref, h0i_ref, a_ref, bz_ref, cb_ref, d_ref, z_ref, hre_ref, him_ref,
                       xr, xi, *, t_new):
    u = u_ref[...]
    ub = u.astype(BF16)
    nb = h0r_ref.shape[0]
    n_tiles = SSM_LANES // LANES
    x_re = jnp.dot(ub, bz_ref[0], preferred_element_type=F32)
    x_im = jnp.dot(ub, bz_ref[1], preferred_element_type=F32)
    for j in range(n_tiles):
        xr[j] = x_re[:, j * LANES:(j + 1) * LANES]
        xi[j] = x_im[:, j * LANES:(j + 1) * LANES]
    for j in range(n_tiles):
        cs = slice(j * LANES, (j + 1) * LANES)
        ar = jnp.broadcast_to(a_ref[0, 0:1, cs], (nb, LANES))
        ai = jnp.broadcast_to(a_ref[1, 0:1, cs], (nb, LANES))
        hr, hi = h0r_ref[:, cs], h0i_ref[:, cs]
        for t in range(t_new):
            rows = pl.ds(t, nb, stride=t_new)
            nhr = ar * hr - ai * hi + xr[j, rows, :]
            nhi = ar * hi + ai * hr + xi[j, rows, :]
            xr[j, rows, :] = nhr
            xi[j, rows, :] = nhi
            hr, hi = nhr, nhi
        hre_ref[:, cs] = hr
        him_ref[:, cs] = hi
    h_re = jnp.concatenate([xr[j] for j in range(n_tiles)], axis=1).astype(BF16)
    h_im = jnp.concatenate([xi[j] for j in range(n_tiles)], axis=1).astype(BF16)
    y = (lax.dot_general(h_re, cb_ref[0], _NT, preferred_element_type=F32)
         - lax.dot_general(h_im, cb_ref[1], _NT, preferred_element_type=F32)
         + d_ref[...] * u)
    z_ref[...] = _gelu(y)


def _ssm_sample(u, h0r, h0i, a, bz, cb, d_row, t_new):
    n_tok = u.shape[0]
    nb = h0r.shape[0]
    slab = pltpu.VMEM((SSM_LANES // LANES, n_tok, LANES), F32)
    st = jax.ShapeDtypeStruct((nb, SSM_LANES), F32)
    return pl.pallas_call(
        functools.partial(_ssm_sample_kernel, t_new=t_new),
        out_shape=[jax.ShapeDtypeStruct((n_tok, SSM_WIDTH), F32), st, st],
        scratch_shapes=[slab, slab],
        compiler_params=pltpu.CompilerParams(vmem_limit_bytes=VMEM_LIMIT),
        name="ssm_sample",
    )(u, h0r, h0i, a, bz, cb, d_row)


def _tail_kernel(x_ref, att_ref, z_ref, p_ref, wglu_ref, bglu_ref, gatt_ref, gssm_ref, wout_ref, bout_ref,
                 g1_ref, b1_ref, wple_ref, wgate_ref, h_ref, ht_ref, ple_ref):
    z = z_ref[...]
    gate = jnp.dot(z.astype(BF16), wglu_ref[...], preferred_element_type=F32) + bglu_ref[...]
    ssm_out = z * _sigmoid(gate)
    att = att_ref[...]
    rms_a = att * lax.rsqrt(jnp.mean(att * att, axis=-1, keepdims=True) + LN_EPS) * gatt_ref[...]
    rms_s = ssm_out * lax.rsqrt(jnp.mean(ssm_out * ssm_out, axis=-1, keepdims=True) + LN_EPS) * gssm_ref[...]
    mix = (jnp.dot(rms_a.astype(BF16), wout_ref[:ATT_WIDTH, :], preferred_element_type=F32)
           + jnp.dot(rms_s.astype(BF16), wout_ref[ATT_WIDTH:, :], preferred_element_type=F32)
           + bout_ref[...])
    pre = ALPHA * x_ref[...] + mix
    mu = jnp.mean(pre, axis=-1, keepdims=True)
    cen = pre - mu
    var = jnp.mean(cen * cen, axis=-1, keepdims=True)
    h = cen * lax.rsqrt(var + LN_EPS) * g1_ref[...] + b1_ref[...]
    hb = h.astype(BF16)
    h_ref[...] = h
    ht_ref[...] = h.T.astype(BF16)
    gate2 = _sigmoid(jnp.dot(hb, wgate_ref[...], preferred_element_type=F32))
    ple_ref[...] = gate2 * jnp.dot(p_ref[...].astype(BF16), wple_ref[...], preferred_element_type=F32)


def _tail(x3, att3, z_arr, p3, weights, z_time_major):
    B, S, D = x3.shape
    tm = _MXU_COLS
    tok = lambda w: pl.BlockSpec((None, tm, w), lambda b, t: (b, t, 0))
    if z_time_major:
        z_arr = z_arr.reshape(B // SUBLANES, S, SUBLANES * SSM_WIDTH)
        z_spec = pl.BlockSpec((None, tm, SSM_WIDTH), lambda b, t: (b // SUBLANES, t, b % SUBLANES))
    else:
        z_spec = tok(SSM_WIDTH)
    full = lambda a: pl.BlockSpec(a.shape, lambda b, t: (0, 0))
    return pl.pallas_call(
        _tail_kernel,
        grid=(B, S // tm),
        in_specs=[tok(D), tok(ATT_WIDTH), z_spec, tok(PLE_DIM)] + [full(w) for w in weights],
        out_specs=[tok(D), pl.BlockSpec((None, D, tm), lambda b, t: (b * (S // tm) + t, 0, 0)), tok(D)],
        out_shape=[jax.ShapeDtypeStruct((B, S, D), F32), jax.ShapeDtypeStruct((B * S // tm, D, tm), BF16),
                   jax.ShapeDtypeStruct((B, S, D), F32)],
        compiler_params=_cparams("parallel", "parallel"),
        name="tail",
    )(x3, att3, z_arr, p3, *weights)


def _split_bf16(x):
    hi = x.astype(BF16)
    lo = (x - hi.astype(F32)).astype(BF16)
    return hi, lo


def _peer_prep_kernel(wpq_ref, sk1_ref, sk2_ref, m1_ref, m2_ref):
    def combined(sk, w):
        sk_hi, sk_lo = _split_bf16(sk)
        w_hi, w_lo = _split_bf16(w)
        dot = lambda a, b: lax.dot_general(a, b, _NT, preferred_element_type=F32)
        return dot(sk_hi, w_hi) + (dot(sk_hi, w_lo) + dot(sk_lo, w_hi))

    half = D_KEY // 2
    m1_ref[...] = combined(sk1_ref[...], wpq_ref[:, :half]).astype(BF16)
    m2_ref[...] = combined(sk2_ref[...], wpq_ref[:, half:]).astype(BF16)


def _peer_prep(w_pq, sk1, sk2):
    half = D_KEY // 2
    m1, m2 = pl.pallas_call(
        _peer_prep_kernel,
        grid=(PEER_HEADS,),
        in_specs=[pl.BlockSpec((D_MODEL, D_KEY), lambda h: (0, h)),
                  pl.BlockSpec((N_KEYS, half), lambda h: (0, 0)),
                  pl.BlockSpec((N_KEYS, half), lambda h: (0, 0))],
        out_specs=[pl.BlockSpec((N_KEYS, D_MODEL), lambda h: (0, h)),
                   pl.BlockSpec((None, N_KEYS, D_MODEL), lambda h: (h, 0, 0))],
        out_shape=[jax.ShapeDtypeStruct((N_KEYS, PEER_HEADS * D_MODEL), BF16),
                   jax.ShapeDtypeStruct((PEER_HEADS, N_KEYS, D_MODEL), BF16)],
        compiler_params=_cparams("parallel"),
        name="peer_prep",
    )(w_pq, sk1, sk2)
    return jnp.concatenate([m1.reshape(N_KEYS * PEER_HEADS, D_MODEL),
                            m2.reshape(PEER_HEADS * N_KEYS, D_MODEL)], axis=0)


def _vmax(a, b):
    return b if a is None else a if b is None else jnp.maximum(a, b)


def _vmin(a, b):
    return None if a is None or b is None else jnp.minimum(a, b)


def _larger_first(xs, i, l):
    a, b = xs[i], xs[l]
    xs[i], xs[l] = _vmax(a, b), _vmin(a, b)


def _sort_desc(xs):
    xs = list(xs)
    n, k = len(xs), 2
    while k <= n:
        j = k // 2
        while j >= 1:
            for i in range(n):
                l = i ^ j
                if l > i:
                    if (i & k) == 0:
                        _larger_first(xs, i, l)
                    else:
                        _larger_first(xs, l, i)
            j //= 2
        k *= 2
    return xs


def _merge_top(a, b):
    n = len(a)
    c = [_vmax(a[i], b[n - 1 - i]) for i in range(n)]
    j = n // 2
    while j >= 1:
        for i in range(n):
            l = i ^ j
            if l > i:
                _larger_first(c, i, l)
        j //= 2
    return c


def _top_of_lists(lists):
    lists = [l + [None] * (PEER_TOPK - len(l)) for l in lists]
    while len(lists) > 1:
        nxt = [_merge_top(lists[i], lists[i + 1]) for i in range(0, len(lists) - 1, 2)]
        if len(lists) % 2:
            nxt.append(lists[-1])
        lists = nxt
    return lists[0]


_CAND_PAIRS = [(a, b) for a in range(PEER_TOPK) for b in range(PEER_TOPK) if (a + 1) * (b + 1) <= PEER_TOPK]
_CAND_SOLO_ROWS = 5


def _peer_gate_kernel(ht_ref, mt_ref, e1_ref, e2_ref, tau_ref, *, tn):
    st = jnp.dot(mt_ref[...], ht_ref[...], preferred_element_type=F32)
    n1 = N_KEYS * PEER_HEADS
    rowid = lax.broadcasted_iota(jnp.int32, (SUBLANES, LANES), 0)
    for jt in range(tn // LANES):
        cs = slice(jt * LANES, (jt + 1) * LANES)
        s1 = [st[i * PEER_HEADS:(i + 1) * PEER_HEADS, cs] for i in range(N_KEYS)]
        top1 = _top_of_lists([_sort_desc(s1[g:g + PEER_TOPK]) for g in range(0, N_KEYS, PEER_TOPK)])
        max1 = top1[0]
        for i in range(N_KEYS):
            e1_ref[jt, i * PEER_HEADS:(i + 1) * PEER_HEADS, :] = jnp.exp(s1[i] - max1)
        v1 = [jnp.exp(t - max1) for t in top1]
        e2_heads, v2_heads = [], []
        for h in range(PEER_HEADS):
            s2 = st[n1 + h * N_KEYS:n1 + (h + 1) * N_KEYS, cs]
            top = _sort_desc([s2[r * SUBLANES:(r + 1) * SUBLANES] for r in range(N_KEYS // SUBLANES)])
            for shift in (4, 2, 1):
                top = _merge_top(top, [pltpu.roll(t, shift, 0) for t in top])
            max2 = top[0]
            e2_heads.append(jnp.exp(s2 - max2[0:1, :]))
            v2_heads.append([jnp.exp(t - max2) for t in top])
        v2 = []
        for b in range(PEER_TOPK):
            acc = v2_heads[0][b]
            for h in range(1, PEER_HEADS):
                acc = jnp.where(rowid == h, v2_heads[h][b], acc)
            v2.append(acc)
        cand0 = {(a, b): v1[a] * v2[b] for a, b in _CAND_PAIRS}
        lists = [[cand0[(a, b)] for b in range(PEER_TOPK) if (a, b) in cand0] for a in range(_CAND_SOLO_ROWS)]
        lists += [[cand0[(a, b)] for a in range(_CAND_SOLO_ROWS, PEER_TOPK) if (a, b) in cand0]
                  for b in range(PEER_TOPK // (_CAND_SOLO_ROWS + 1))]
        assert sum(len(l) for l in lists) == len(_CAND_PAIRS)
        top = _top_of_lists(lists)
        tau0 = top[PEER_TOPK - 1]
        rz = 1.0 / functools.reduce(lambda x, y: x + y, top)
        v2n = [v * rz for v in v2]
        tau = None
        for (a, b), c0 in cand0.items():
            c = jnp.where(c0 >= tau0, v1[a] * v2n[b], jnp.inf)
            tau = c if tau is None else jnp.minimum(tau, c)
        tau_ref[jt] = tau
        for h in range(PEER_HEADS):
            e2_ref[jt, h * N_KEYS:(h + 1) * N_KEYS, :] = e2_heads[h] * rz[h:h + 1, :]


def _peer_gate(ht, mt):
    tn = _MXU_COLS
    n_tok = ht.shape[0] * tn
    nt = tn // LANES
    slab = lambda rows: pl.BlockSpec((nt, rows, LANES), lambda t: (t, 0, 0))
    return pl.pallas_call(
        functools.partial(_peer_gate_kernel, tn=tn),
        grid=(n_tok // tn,),
        in_specs=[pl.BlockSpec((None, D_MODEL, tn), lambda t: (t, 0, 0)),
                  pl.BlockSpec(mt.shape, lambda t: (0, 0))],
        out_specs=[slab(N_KEYS * PEER_HEADS), slab(N_KEYS * PEER_HEADS), slab(SUBLANES)],
        out_shape=[jax.ShapeDtypeStruct((n_tok // LANES, N_KEYS * PEER_HEADS, LANES), F32),
                   jax.ShapeDtypeStruct((n_tok // LANES, N_KEYS * PEER_HEADS, LANES), F32),
                   jax.ShapeDtypeStruct((n_tok // LANES, SUBLANES, LANES), F32)],
        compiler_params=_cparams("parallel"),
        name="peer_gate",
    )(ht, mt)


def _gelu_times(x, g):
    c = math.sqrt(2.0 / math.pi)
    hx = 0.5 * x
    t = jnp.tanh(x * (x * x * (c * 0.044715) + c))
    return g * (hx + hx * t)


_KEYS_PER_SLICE = 2


def _peer_dense_kernel(ht_ref, u_ref, vt_ref, e1_ref, e2_ref, tau_ref, o_ref, acc_t, w_even, w_odd, *,
                       tb, ec, n_chunks):
    s = pl.program_id(1)
    n_pairs = tb // _MXU_COLS
    keys_per_chunk = ec // N_KEYS
    slices_per_pair = keys_per_chunk // _KEYS_PER_SLICE
    n_iter = n_pairs * slices_per_pair
    e_rows = _KEYS_PER_SLICE * N_KEYS
    v_rows = D_MODEL // slices_per_pair
    chunk = jnp.minimum(s, n_chunks - 1)

    @pl.when(s == 0)
    def _():
        acc_t[...] = jnp.zeros_like(acc_t)
        w_odd[...] = jnp.zeros_like(w_odd)

    def scores(it):
        pr, sl = it // slices_per_pair, it % slices_per_pair
        return jnp.dot(u_ref[sl * e_rows:(sl + 1) * e_rows, :], ht_ref[pr], preferred_element_type=F32)

    def step(w_in, w_out):
        a_next = scores(0)
        for it in range(n_iter):
            pr, sl = it // slices_per_pair, it % slices_per_pair
            a = a_next
            if it + 1 < n_iter:
                a_next = scores(it + 1)
            vrows = slice(sl * v_rows, (sl + 1) * v_rows)
            acc_t[pr, vrows, :] += jnp.dot(vt_ref[vrows, :], w_in[pr], preferred_element_type=F32)
            halves = []
            for half in range(_MXU_COLS // LANES):
                j = pr * (_MXU_COLS // LANES) + half
                cs = slice(half * LANES, (half + 1) * LANES)
                taub = [jnp.broadcast_to(tau_ref[j, pl.ds(h, 1), :], (SUBLANES, LANES)) for h in range(PEER_HEADS)]
                pieces = []
                for kk in range(_KEYS_PER_SLICE):
                    k1 = chunk * keys_per_chunk + sl * _KEYS_PER_SLICE + kk
                    e1b = [jnp.broadcast_to(e1_ref[j, pl.ds(k1 * PEER_HEADS + h, 1), :], (SUBLANES, LANES))
                           for h in range(PEER_HEADS)]
                    for kb in range(N_KEYS // SUBLANES):
                        gate = None
                        for h in range(PEER_HEADS):
                            prod = e1b[h] * e2_ref[j, pl.ds(h * N_KEYS + kb * SUBLANES, SUBLANES), :]
                            sel = jnp.where(prod >= taub[h], prod, 0.0)
                            gate = sel if gate is None else gate + sel
                        r0 = kk * N_KEYS + kb * SUBLANES
                        pieces.append(_gelu_times(a[r0:r0 + SUBLANES, cs], gate))
                halves.append(jnp.concatenate(pieces, axis=0))
            w_out[pr, sl * e_rows:(sl + 1) * e_rows, :] = jnp.concatenate(halves, axis=1).astype(BF16)

    @pl.when(s % 2 == 0)
    def _():
        step(w_odd, w_even)

    @pl.when(s % 2 == 1)
    def _():
        step(w_even, w_odd)

    @pl.when(s == n_chunks)
    def _():
        for pr in range(n_pairs):
            o_ref[pr * _MXU_COLS:(pr + 1) * _MXU_COLS, :] = acc_t[pr].T


def _peer_dense(ht, u_bf, vt_bf, e1, e2, tau, tb=1024, ec=512):
    n_tok = ht.shape[0] * _MXU_COLS
    tb = min(tb, n_tok)
    nt = tb // LANES
    n_chunks = N_EXPERTS // ec
    slab = lambda rows: pl.BlockSpec((nt, rows, LANES), lambda t, s: (t, 0, 0))
    w_buf = pltpu.VMEM((tb // _MXU_COLS, ec, _MXU_COLS), BF16)
    return pl.pallas_call(
        functools.partial(_peer_dense_kernel, tb=tb, ec=ec, n_chunks=n_chunks),
        grid=(n_tok // tb, n_chunks + 1),
        in_specs=[pl.BlockSpec((tb // _MXU_COLS, D_MODEL, _MXU_COLS), lambda t, s: (t, 0, 0)),
                  pl.BlockSpec((ec, D_MODEL), lambda t, s: (jnp.minimum(s, n_chunks - 1), 0)),
                  pl.BlockSpec((D_MODEL, ec), lambda t, s: (0, jnp.maximum(s - 1, 0))),
                  slab(N_KEYS * PEER_HEADS), slab(N_KEYS * PEER_HEADS), slab(SUBLANES)],
        out_specs=pl.BlockSpec((tb, D_MODEL), lambda t, s: (t, 0)),
        out_shape=jax.ShapeDtypeStruct((n_tok, D_MODEL), F32),
        scratch_shapes=[pltpu.VMEM((tb // _MXU_COLS, D_MODEL, _MXU_COLS), F32), w_buf, w_buf],
        compiler_params=_cparams("parallel", "arbitrary"),
        name="peer_dense",
    )(ht, u_bf, vt_bf, e1, e2, tau)


def _final_kernel(h_ref, f_ref, p_ref, g_ref, b_ref, o_ref):
    pre = ALPHA * h_ref[...] + f_ref[...] + p_ref[...]
    mu = jnp.mean(pre, axis=-1, keepdims=True)
    cen = pre - mu
    var = jnp.mean(cen * cen, axis=-1, keepdims=True)
    o_ref[...] = cen * lax.rsqrt(var + LN_EPS) * g_ref[...] + b_ref[...]


def _final(h2, ffn2, ple2, g, b, tm=512):
    n_tok, D = h2.shape
    tm = min(tm, n_tok)
    tok = pl.BlockSpec((tm, D), lambda t: (t, 0))
    vec = pl.BlockSpec((1, D), lambda t: (0, 0))
    return pl.pallas_call(
        _final_kernel,
        grid=(n_tok // tm,),
        in_specs=[tok, tok, tok, vec, vec],
        out_specs=tok,
        out_shape=jax.ShapeDtypeStruct((n_tok, D), F32),
        compiler_params=_cparams("parallel"),
        name="final_norm",
    )(h2, ffn2, ple2, g, b)


def _ffn_and_norm(h3, ht, ple3, mt, u_bf, vt_bf, ln2_g, ln2_b):
    B, S, D = h3.shape
    n_tok = B * S
    e1, e2, tau = _peer_gate(ht, mt)
    ffn = _peer_dense(ht, u_bf, vt_bf, e1, e2, tau)
    y = _final(h3.reshape(n_tok, D), ffn, ple3.reshape(n_tok, D), ln2_g, ln2_b)
    return y.reshape(B, S, D)


def kernel(x_prompt, x_sample, cache_k, cache_v, state_ssm_re, state_ssm_im, p_prompt, p_sample,
           w_in, b_in, lam_re, lam_im, log_dt, ssm_b_re, ssm_b_im, ssm_c_re, ssm_c_im, ssm_d,
           w_glu, b_glu, g_att_out, g_ssm_out, w_out, b_out, ln1_g, ln1_b,
           w_pq, sub_keys1, sub_keys2, peer_u, peer_v, w_ple, w_ple_gate, ln2_g, ln2_b):
    assert w_in.shape[0] == DEPTH == 1
    B, S, D = x_prompt.shape
    Bs, Ts, _ = x_sample.shape
    row = lambda t: t.reshape(1, -1)

    w_in_bf = w_in[0].astype(BF16)
    b_in2 = row(b_in[0])
    a, bz, cb = _ssm_prep(lam_re[0], lam_im[0], log_dt[0], ssm_b_re[0], ssm_b_im[0], ssm_c_re[0], ssm_c_im[0])
    d_row = row(ssm_d[0])
    tail_w = (w_glu[0].astype(BF16), row(b_glu[0]), row(g_att_out[0]), row(g_ssm_out[0]),
              w_out[0].astype(BF16), row(b_out[0]), row(ln1_g[0]), row(ln1_b[0]),
              w_ple[0].astype(BF16), w_ple_gate[0].astype(BF16))
    mt = _peer_prep(w_pq[0], sub_keys1[0], sub_keys2[0])
    u_bf = peer_u[0].astype(BF16)
    vt_bf = peer_v[0].astype(BF16).T
    g2, b2 = row(ln2_g[0]), row(ln2_b[0])

    q, k_p, v_p, u_tm, kt_p, vt_p = _in_proj(x_prompt, w_in_bf, b_in2, prompt=True)
    att = _attn_prompt(q, k_p, v_p)
    z_tm, hre_p, him_p = _ssm_prompt(u_tm.reshape(B // SUBLANES, S * SUBLANES, SSM_WIDTH), a, bz, cb, d_row)
    h3, hb3, ple3 = _tail(x_prompt, att, z_tm, p_prompt[0], tail_w, z_time_major=True)
    y_p = _ffn_and_norm(h3, hb3, ple3, mt, u_bf, vt_bf, g2, b2)

    n_s = Bs * Ts
    q_s, k_s, v_s, u_s = _in_proj(x_sample.reshape(1, n_s, D), w_in_bf, b_in2, prompt=False)
    per_seq = lambda t: t.reshape(Bs, Ts, ATT_WIDTH)
    att_s = _attn_sample(per_seq(q_s), per_seq(k_s), per_seq(v_s), cache_k[0], cache_v[0])
    z_s, hre_s, him_s = _ssm_sample(u_s.reshape(n_s, SSM_WIDTH), state_ssm_re[0].reshape(Bs, SSM_LANES),
                                    state_ssm_im[0].reshape(Bs, SSM_LANES), a, bz, cb, d_row, Ts)
    h3s, hb3s, ple3s = _tail(x_sample.reshape(1, n_s, D), att_s.reshape(1, n_s, ATT_WIDTH),
                             z_s.reshape(1, n_s, SSM_WIDTH), p_sample[0].reshape(1, n_s, PLE_DIM), tail_w,
                             z_time_major=False)
    y_s = _ffn_and_norm(h3s, hb3s, ple3s, mt, u_bf, vt_bf, g2, b2).reshape(Bs, Ts, D)

    w_keep = min(BRANCHES[-1][0], S)
    heads = lambda t, b, s: t.reshape(1, b, s, N_HEADS, HEAD_DIM)
    window = lambda t: (t.reshape(B, N_HEADS, HEAD_DIM, S)[..., S - w_keep:].transpose(0, 3, 1, 2)
                        .reshape(1, B, w_keep, N_HEADS, HEAD_DIM))
    state = lambda t, b: t.reshape(1, b, N_SSM_GROUPS, SSM_STATE)
    return (y_p, y_s,
            window(kt_p), window(vt_p),
            state(hre_p, B), state(him_p, B),
            heads(k_s, Bs, Ts), heads(v_s, Bs, Ts),
            state(hre_s, Bs), state(him_s, Bs))
```
